```python
import jax, jax.numpy as jnp
from jax import lax
import numpy as np

D_MODEL = 1024
BATCH = 2
SEQ = 8192
DEPTH = 4
DEC_BATCH = 128
DEC_SEQ = 4
PAST_LEN = 8192
PAGE_SIZE = 128

N_HEADS = 8
NOPE_DIM = D_MODEL // 16
ROPE_DIM = D_MODEL // 32
V_DIM = D_MODEL // 16
Q_LORA = 12 * NOPE_DIM
KV_LORA = 4 * NOPE_DIM
ROPE_BASE = 10000.0
ATTN_SCALE = (NOPE_DIM + ROPE_DIM) ** -0.5
Q_BLOCK = 128
CONV_CH = D_MODEL // 2
CONV_WIDTH = 31
POOL_WINDOWS = (2, 4, 8, 16)
N_POOL_GROUPS = len(POOL_WINDOWS)
POOL_GROUP = D_MODEL // N_POOL_GROUPS
POOL_STATE = max(POOL_WINDOWS) - 1
D_FF = ((8 * D_MODEL // 3 + 127) // 128) * 128
EPS = 1e-6
IN_WIDTH = Q_LORA + KV_LORA + ROPE_DIM + 2 * CONV_CH
MIX_WIDTH = N_HEADS * V_DIM + CONV_CH
N_EVEN = (DEPTH + 1) // 2
N_ODD = DEPTH // 2

kernel_name = 'hybrid_mla_conv_pool_decoder'


def rmsnorm(x, g):
    xf = x.astype(jnp.float32)
    y = xf * lax.rsqrt(jnp.mean(xf * xf, -1, keepdims=True) + EPS)
    return (y * g.astype(jnp.float32)).astype(x.dtype)


def layernorm(x, g, b):
    xf = x.astype(jnp.float32)
    xc = xf - jnp.mean(xf, -1, keepdims=True)
    var = jnp.mean(xc * xc, -1, keepdims=True)
    return (xc * lax.rsqrt(var + EPS) * g.astype(jnp.float32) + b.astype(jnp.float32)).astype(x.dtype)


def swiglu(x, wg, wu, wd):
    return (jax.nn.silu(x @ wg) * (x @ wu)) @ wd


def rope_tables(pos):
    inv = ROPE_BASE ** (-jnp.arange(0, ROPE_DIM, 2, dtype=jnp.float32) / ROPE_DIM)
    ang = pos[:, None] * inv[None, :]
    return jnp.cos(ang), jnp.sin(ang)


def apply_rope(x, cos, sin):
    xf = x.astype(jnp.float32)
    half = ROPE_DIM // 2
    x1, x2 = xf[..., :half], xf[..., half:]
    return jnp.concatenate([x1 * cos - x2 * sin, x2 * cos + x1 * sin], -1).astype(x.dtype)


def mla_project(h, cos, sin, q_norm, w_uq, kv_norm, w_uk):
    q_lat = h[..., :Q_LORA]
    kv_lat = h[..., Q_LORA:Q_LORA + KV_LORA]
    kr = h[..., Q_LORA + KV_LORA:Q_LORA + KV_LORA + ROPE_DIM]
    conv_in = h[..., Q_LORA + KV_LORA + ROPE_DIM:]
    q = jnp.einsum('btq,qhd->bthd', rmsnorm(q_lat, q_norm), w_uq)
    q_nope, q_rope = q[..., :NOPE_DIM], q[..., NOPE_DIM:]
    q_rope = apply_rope(q_rope, cos[:, None, :], sin[:, None, :])
    q_abs = jnp.einsum('bthd,chd->bthc', q_nope, w_uk)
    c_kv = rmsnorm(kv_lat, kv_norm)
    k_rope = apply_rope(kr, cos, sin)
    return q_abs, q_rope, c_kv, k_rope, conv_in


def mla_attend_prompt(q_abs, q_rope, c_kv, k_rope):
    B, T = q_abs.shape[:2]
    nb = T // Q_BLOCK
    qa = q_abs.reshape(B, nb, Q_BLOCK, N_HEADS, KV_LORA).transpose(1, 0, 2, 3, 4)
    qr = q_rope.reshape(B, nb, Q_BLOCK, N_HEADS, ROPE_DIM).transpose(1, 0, 2, 3, 4)
    kpos = jnp.arange(T)

    def block(args):
        qa_b, qr_b, i = args
        s = (jnp.einsum('bqhc,bkc->bhqk', qa_b, c_kv)
             + jnp.einsum('bqhr,bkr->bhqk', qr_b, k_rope)).astype(jnp.float32) * ATTN_SCALE
        qpos = i * Q_BLOCK + jnp.arange(Q_BLOCK)
        s = jnp.where(kpos[None, :] <= qpos[:, None], s, -jnp.inf)
        p = jax.nn.softmax(s, -1).astype(c_kv.dtype)
        return jnp.einsum('bhqk,bkc->bqhc', p, c_kv)

    o = lax.map(block, (qa, qr, jnp.arange(nb)))
    return o.transpose(1, 0, 2, 3, 4).reshape(B, T, N_HEADS, KV_LORA)


def mla_attend_sample(q_abs, q_rope, c_kv, k_rope, lat_past, kr_past):
    TS = q_abs.shape[1]
    P = lat_past.shape[1]
    s_past = (jnp.einsum('bthc,bkc->bhtk', q_abs, lat_past)
              + jnp.einsum('bthr,bkr->bhtk', q_rope, kr_past)).astype(jnp.float32)
    s_new = (jnp.einsum('bthc,bkc->bhtk', q_abs, c_kv)
             + jnp.einsum('bthr,bkr->bhtk', q_rope, k_rope)).astype(jnp.float32)
    causal = jnp.tril(jnp.ones((TS, TS), dtype=bool))
    s_new = jnp.where(causal, s_new, -jnp.inf)
    p = jax.nn.softmax(jnp.concatenate([s_past, s_new], -1) * ATTN_SCALE, -1).astype(c_kv.dtype)
    return (jnp.einsum('bhtk,bkc->bthc', p[..., :P], lat_past)
            + jnp.einsum('bhtk,bkc->bthc', p[..., P:], c_kv))


def conformer_conv(conv_in, hist, w, b, ln_g, ln_b):
    a, g = conv_in[..., :CONV_CH], conv_in[..., CONV_CH:]
    u = a * jax.nn.sigmoid(g)
    full = jnp.concatenate([hist, u], 1)
    y = lax.conv_general_dilated(full, w[:, None, :], window_strides=(1,), padding='VALID',
                                 dimension_numbers=('NWC', 'WIO', 'NWC'),
                                 feature_group_count=CONV_CH) + b
    y = jax.nn.silu(layernorm(y, ln_g, ln_b))
    return y, full[:, -(CONV_WIDTH - 1):]


def pool_mixer(h, hist, pos0, pool_w, pool_scale):
    B, T, D = h.shape
    S = POOL_STATE
    full = jnp.concatenate([hist, h], 1)
    ff = full.astype(jnp.float32)
    cs = jnp.concatenate([jnp.zeros((B, 1, D), jnp.float32), jnp.cumsum(ff, 1)], 1)
    cnt_pos = pos0 + jnp.arange(T) + 1
    outs = []
    for gi, w in enumerate(POOL_WINDOWS):
        sl = slice(gi * POOL_GROUP, (gi + 1) * POOL_GROUP)
        s = cs[:, S + 1:S + 1 + T, sl] - cs[:, S + 1 - w:S + 1 - w + T, sl]
        cnt = jnp.minimum(cnt_pos, w).astype(jnp.float32)[None, :, None]
        outs.append(s / cnt)
    diff = (jnp.concatenate(outs, -1) - ff[:, S:]).astype(h.dtype)
    y = jnp.einsum('btgc,gcd->btgd', diff.reshape(B, T, N_POOL_GROUPS, POOL_GROUP), pool_w)
    return y.reshape(B, T, D) * pool_scale, full[:, -S:]


def setup_inputs(seed: int = 0) -> dict:
    key = jax.random.key(seed)
    keys = iter(jax.random.split(key, 64))

    def nrm(shape, scale):
        return jax.random.normal(next(keys), shape, jnp.float32) * scale

    def gain(shape):
        return 1.0 + 0.05 * jax.random.normal(next(keys), shape, jnp.float32)

    n_pages = PAST_LEN // PAGE_SIZE
    n_used = DEC_BATCH * n_pages
    n_phys = (5 * n_used + 3) // 4
    perm = jax.random.permutation(next(keys), n_phys)
    page_table = perm[:n_used].reshape(DEC_BATCH, n_pages).astype(jnp.int32)
    ne, no = N_EVEN, N_ODD
    return {
        'x_prompt': nrm((BATCH, SEQ, D_MODEL), 1.0),
        'x_sample': nrm((DEC_BATCH, DEC_SEQ, D_MODEL), 1.0),
        'cache_latent': nrm((ne, n_phys, PAGE_SIZE, KV_LORA), 1.0),
        'cache_krope': nrm((ne, n_phys, PAGE_SIZE, ROPE_DIM), 1.0),
        'state_conv': nrm((ne, DEC_BATCH, CONV_WIDTH - 1, CONV_CH), 0.5),
        'state_pool': nrm((no, DEC_BATCH, POOL_STATE, D_MODEL), 1.0),
        'page_table': page_table,
        'ffn1_norm': gain((DEPTH, D_MODEL)),
        'ffn1_w_gate': nrm((DEPTH, D_MODEL, D_FF), D_MODEL ** -0.5),
        'ffn1_w_up': nrm((DEPTH, D_MODEL, D_FF), D_MODEL ** -0.5),
        'ffn1_w_down': nrm((DEPTH, D_FF, D_MODEL), D_FF ** -0.5),
        'mix_norm': gain((DEPTH, D_MODEL)),
        'w_in': nrm((ne, D_MODEL, IN_WIDTH), D_MODEL ** -0.5),
        'q_norm': gain((ne, Q_LORA)),
        'w_uq': nrm((ne, Q_LORA, N_HEADS, NOPE_DIM + ROPE_DIM), Q_LORA ** -0.5),
        'kv_norm': gain((ne, KV_LORA)),
        'w_uk': nrm((ne, KV_LORA, N_HEADS, NOPE_DIM), KV_LORA ** -0.5),
        'w_uv': nrm((ne, KV_LORA, N_HEADS, V_DIM), KV_LORA ** -0.5),
        'conv_w': nrm((ne, CONV_WIDTH, CONV_CH), CONV_WIDTH ** -0.5),
        'conv_b': nrm((ne, CONV_CH), 0.02),
        'conv_ln_g': gain((ne, CONV_CH)),
        'conv_ln_b': nrm((ne, CONV_CH), 0.02),
        'w_out': nrm((ne, MIX_WIDTH, D_MODEL), MIX_WIDTH ** -0.5),
        'pool_w': nrm((no, N_POOL_GROUPS, POOL_GROUP, POOL_GROUP), POOL_GROUP ** -0.5),
        'pool_scale': gain((no, D_MODEL)),
        'ffn2_norm': gain((DEPTH, D_MODEL)),
        'ffn2_w_gate': nrm((DEPTH, D_MODEL, D_FF), D_MODEL ** -0.5),
        'ffn2_w_up': nrm((DEPTH, D_MODEL, D_FF), D_MODEL ** -0.5),
        'ffn2_w_down': nrm((DEPTH, D_FF, D_MODEL), D_FF ** -0.5),
        'final_norm': gain((D_MODEL,)),
    }


def reference(x_prompt, x_sample, cache_latent, cache_krope, state_conv, state_pool, page_table,
              ffn1_norm, ffn1_w_gate, ffn1_w_up, ffn1_w_down,
              mix_norm, w_in, q_norm, w_uq, kv_norm, w_uk, w_uv,
              conv_w, conv_b, conv_ln_g, conv_ln_b, w_out,
              pool_w, pool_scale,
              ffn2_norm, ffn2_w_gate, ffn2_w_up, ffn2_w_down, final_norm):
    B, T = x_prompt.shape[:2]
    DB, TS = x_sample.shape[:2]
    n_pages = page_table.shape[1]
    past_len = n_pages * cache_latent.shape[2]
    cos_p, sin_p = rope_tables(jnp.arange(T, dtype=jnp.float32))
    cos_s, sin_s = rope_tables(past_len + jnp.arange(TS, dtype=jnp.float32))
    conv_hist0 = jnp.zeros((B, CONV_WIDTH - 1, CONV_CH), x_prompt.dtype)
    pool_hist0 = jnp.zeros((B, POOL_STATE, D_MODEL), x_prompt.dtype)

    xp, xs = x_prompt, x_sample
    lat_p, kr_p, lat_s, kr_s = [], [], [], []
    conv_p, conv_s, pool_p, pool_s = [], [], [], []
    for l in range(DEPTH):
        xp = xp + 0.5 * swiglu(rmsnorm(xp, ffn1_norm[l]), ffn1_w_gate[l], ffn1_w_up[l], ffn1_w_down[l])
        xs = xs + 0.5 * swiglu(rmsnorm(xs, ffn1_norm[l]), ffn1_w_gate[l], ffn1_w_up[l], ffn1_w_down[l])
        if l % 2 == 0:
            e = l // 2
            hp = rmsnorm(xp, mix_norm[l]) @ w_in[e]
            hs = rmsnorm(xs, mix_norm[l]) @ w_in[e]
            qa_p, qr_p, ckv_p, kro_p, cin_p = mla_project(hp, cos_p, sin_p, q_norm[e], w_uq[e], kv_norm[e], w_uk[e])
            qa_s, qr_s, ckv_s, kro_s, cin_s = mla_project(hs, cos_s, sin_s, q_norm[e], w_uq[e], kv_norm[e], w_uk[e])
            o_p = mla_attend_prompt(qa_p, qr_p, ckv_p, kro_p)
            lat_past = cache_latent[e][page_table].reshape(DB, past_len, KV_LORA)
            kr_past = cache_krope[e][page_table].reshape(DB, past_len, ROPE_DIM)
            o_s = mla_attend_sample(qa_s, qr_s, ckv_s, kro_s, lat_past, kr_past)
            att_p = jnp.einsum('bthc,chv->bthv', o_p, w_uv[e]).reshape(B, T, N_HEADS * V_DIM)
            att_s = jnp.einsum('bthc,chv->bthv', o_s, w_uv[e]).reshape(DB, TS, N_HEADS * V_DIM)
            cv_p, hist_p = conformer_conv(cin_p, conv_hist0, conv_w[e], conv_b[e], conv_ln_g[e], conv_ln_b[e])
            cv_s, hist_s = conformer_conv(cin_s, state_conv[e], conv_w[e], conv_b[e], conv_ln_g[e], conv_ln_b[e])
            xp = xp + jnp.concatenate([att_p, cv_p], -1) @ w_out[e]
            xs = xs + jnp.concatenate([att_s, cv_s], -1) @ w_out[e]
            lat_p.append(ckv_p); kr_p.append(kro_p)
            lat_s.append(ckv_s); kr_s.append(kro_s)
            conv_p.append(hist_p); conv_s.append(hist_s)
        else:
            o = l // 2
            yp, ph = pool_mixer(rmsnorm(xp, mix_norm[l]), pool_hist0, 0, pool_w[o], pool_scale[o])
            ys, sh = pool_mixer(rmsnorm(xs, mix_norm[l]), state_pool[o], past_len, pool_w[o], pool_scale[o])
            xp = xp + yp
            xs = xs + ys
            pool_p.append(ph); pool_s.append(sh)
        xp = xp + 0.5 * swiglu(rmsnorm(xp, ffn2_norm[l]), ffn2_w_gate[l], ffn2_w_up[l], ffn2_w_down[l])
        xs = xs + 0.5 * swiglu(rmsnorm(xs, ffn2_norm[l]), ffn2_w_gate[l], ffn2_w_up[l], ffn2_w_down[l])

    y_prompt = rmsnorm(xp, final_norm)
    y_sample = rmsnorm(xs, final_norm)
    return (y_prompt, y_sample,
            jnp.stack(lat_p), jnp.stack(kr_p), jnp.stack(lat_s), jnp.stack(kr_s),
            jnp.stack(conv_p), jnp.stack(conv_s), jnp.stack(pool_p), jnp.stack(pool_s))
```

```python
import functools

import jax
import jax.numpy as jnp
from jax import lax
from jax.experimental import pallas as pl
from jax.experimental.pallas import tpu as pltpu

EPS = 1e-6
ROPE_BASE = 10000.0
LANES = 128
HEAD_PAD = 128
NEG_BIG = -1e30
VMEM_LIMIT = 56 * 1024 * 1024
BF16 = jnp.bfloat16
F32 = jnp.float32


def _cparams(*sem):
    return pltpu.CompilerParams(dimension_semantics=sem, vmem_limit_bytes=VMEM_LIMIT)


def _resident(shape):
    nd = len(shape)
    return pl.BlockSpec(shape, lambda *_: (0,) * nd, pipeline_mode=pl.Buffered(1))


def _rms(x, g):
    return x * lax.rsqrt(jnp.mean(x * x, axis=-1, keepdims=True) + EPS) * g


def _dot(a, b):
    return jnp.dot(a, b, preferred_element_type=F32)


def _dot_nt(a, b):
    return lax.dot_general(a, b, (((1,), (1,)), ((), ())), preferred_element_type=F32)


def _ffn_body(x_ref, g_ref, wg_ref, wu_ref, wd_ref, fg_ref, o_ref, *, chunk, final):
    x = x_ref[...]
    h = _rms(x, g_ref[...]).astype(BF16)
    d_ff = wg_ref.shape[1]
    acc = jnp.zeros(x.shape, F32)
    for c in range(d_ff // chunk):
        sl = slice(c * chunk, (c + 1) * chunk)
        gate = _dot(h, wg_ref[:, sl])
        up = _dot(h, wu_ref[:, sl])
        act = (gate * jax.nn.sigmoid(gate) * up).astype(BF16)
        acc = acc + _dot(act, wd_ref[sl, :])
    y = x + 0.5 * acc
    if final:
        y = _rms(y, fg_ref[...])
    o_ref[...] = y


def _ffn(x, g, wg, wu, wd, fg, *, final=False):
    n, d = x.shape
    d_ff = wg.shape[1]
    tm = min(512, n)
    chunk = 256 if d_ff % 256 == 0 else LANES
    row = lambda i: (i, 0)
    return pl.pallas_call(
        functools.partial(_ffn_body, chunk=chunk, final=final),
        out_shape=jax.ShapeDtypeStruct((n, d), F32),
        grid=(n // tm,),
        in_specs=[pl.BlockSpec((tm, d), row), _resident((1, d)), _resident((d, d_ff)),
                  _resident((d, d_ff)), _resident((d_ff, d)), _resident((1, d))],
        out_specs=pl.BlockSpec((tm, d), row),
        compiler_params=_cparams("parallel"),
        name="ffn",
    )(x, g, wg, wu, wd, fg)


def _rope(v, t1, t2):
    lane = lax.broadcasted_iota(jnp.int32, v.shape, 1)
    partner = jnp.where(lane < 80, pltpu.roll(v, 112, 1), pltpu.roll(v, 16, 1))
    return v * t1 + partner * t2


def _proj_body(x_ref, t1_ref, t2_ref, g_ref, wq_ref, wkv_ref, wkr_ref, wa_ref, wgl_ref, qn_ref, kvn_ref,
               wuq_ref, wk_ref, *rest, absorbed, n_heads):
    if absorbed:
        q_ref, lat_ref, kr_ref, u_ref, qa_ref = rest
    else:
        wv_ref, q_ref, lat_ref, kr_ref, u_ref, k_ref, v_ref = rest
    x = x_ref[...]
    h = _rms(x, g_ref[...]).astype(BF16)
    t1 = t1_ref[...]
    t2 = t2_ref[...]

    u_ref[...] = _dot(h, wa_ref[...]) * jax.nn.sigmoid(_dot(h, wgl_ref[...]))

    c_kv = _rms(_dot(h, wkv_ref[...]), kvn_ref[...])
    lat_ref[...] = c_kv
    c_bf = c_kv.astype(BF16)

    kr = _rope(_dot(h, wkr_ref[...]), t1, t2)
    kr_ref[...] = kr[:, 64:96]

    qn = _rms(_dot(h, wq_ref[...]), qn_ref[...]).astype(BF16)
    q = _dot(qn, wuq_ref[...])
    q_heads = [_rope(q[:, i * HEAD_PAD:(i + 1) * HEAD_PAD], t1, t2).astype(BF16) for i in range(n_heads)]
    q_ref[...] = jnp.concatenate(q_heads, axis=1)

    if absorbed:
        kv = wk_ref.shape[2]
        for i in range(n_heads):
            qa_ref[:, i * kv:(i + 1) * kv] = _dot(q_heads[i], wk_ref[i]).astype(BF16)
    else:
        k = _dot(c_bf, wk_ref[...])
        k_ref[...] = jnp.concatenate(
            [(k[:, i * HEAD_PAD:(i + 1) * HEAD_PAD] + kr).astype(BF16) for i in range(n_heads)], axis=1)
        v_ref[...] = _dot(c_bf, wv_ref[...]).astype(BF16)


def _proj(x, t1, t2, w, *, absorbed):
    n, d = x.shape
    tm = min(512, n)
    n_heads = w["n_heads"]
    kv = w["w_kv"].shape[1]
    rope = w["rope"]
    conv_ch = w["w_a"].shape[1]
    row = lambda i: (i, 0)
    weights = [w["mix_norm"], w["w_q"], w["w_kv"], w["w_kr"], w["w_a"], w["w_g"], w["q_norm"], w["kv_norm"],
               w["w_uq"]] + ([w["w_uk_abs"]] if absorbed else [w["w_uk_pad"], w["w_uv"]])
    extra_w = [n_heads * kv] if absorbed else [n_heads * HEAD_PAD, w["w_uv"].shape[1]]
    widths = [(n_heads * HEAD_PAD, BF16), (kv, F32), (rope, F32), (conv_ch, F32)] + [(c, BF16) for c in extra_w]
    return pl.pallas_call(
        functools.partial(_proj_body, absorbed=absorbed, n_heads=n_heads),
        out_shape=[jax.ShapeDtypeStruct((n, c), dt) for c, dt in widths],
        grid=(n // tm,),
        in_specs=[pl.BlockSpec((tm, d), row), pl.BlockSpec((tm, LANES), row), pl.BlockSpec((tm, LANES), row)]
                 + [_resident(a.shape) for a in weights],
        out_specs=[pl.BlockSpec((tm, c), row) for c, _ in widths],
        compiler_params=_cparams("parallel"),
        name="proj_abs" if absorbed else "proj",
    )(x, t1, t2, *weights)


def _attn_body(q_ref, k_ref, v_ref, o_ref, *, tq, v_dim):
    qi = pl.program_id(2)
    outs = []
    for hh in range(2):
        cols = slice(hh * HEAD_PAD, (hh + 1) * HEAD_PAD)
        q = q_ref[0, :, cols]

        def step(j, carry, masked):
            m, l, acc = carry
            k = k_ref[0, pl.ds(pl.multiple_of(j * tq, tq), tq), cols]
            v = v_ref[0, pl.ds(pl.multiple_of(j * tq, tq), tq), :]
            s = _dot_nt(q, k)
            if masked:
                r = lax.broadcasted_iota(jnp.int32, s.shape, 0)
                c = lax.broadcasted_iota(jnp.int32, s.shape, 1)
                s = jnp.where(c <= r, s, NEG_BIG)
            m_new = jnp.maximum(m, jnp.max(s, axis=1, keepdims=True))
            p = jnp.exp(s - m_new)
            alpha = jnp.exp(m - m_new)
            l = alpha * l + jnp.sum(p, axis=1, keepdims=True)
            acc = alpha * acc + _dot(p.astype(BF16), v)
            return m_new, l, acc

        init = (jnp.full((tq, 1), NEG_BIG, F32), jnp.zeros((tq, 1), F32), jnp.zeros((tq, 2 * v_dim), F32))
        carry = lax.fori_loop(0, qi, functools.partial(step, masked=False), init)
        m, l, acc = step(qi, carry, True)
        outs.append(acc / l)
    lane = lax.broadcasted_iota(jnp.int32, outs[0].shape, 1)
    o_ref[0] = jnp.where(lane < v_dim, outs[0], outs[1]).astype(o_ref.dtype)


def _attn_prompt(q, k, v, *, n_heads, v_dim):
    b, t, _ = q.shape
    tq = min(512, t)
    return pl.pallas_call(
        functools.partial(_attn_body, tq=tq, v_dim=v_dim),
        out_shape=jax.ShapeDtypeStruct((b, t, n_heads * v_dim), BF16),
        grid=(b, n_heads // 2, t // tq),
        in_specs=[pl.BlockSpec((1, tq, 2 * HEAD_PAD), lambda bi, hp, qi: (bi, qi, hp)),
                  pl.BlockSpec((1, t, 2 * HEAD_PAD), lambda bi, hp, qi: (bi, 0, hp)),
                  pl.BlockSpec((1, t, 2 * v_dim), lambda bi, hp, qi: (bi, 0, hp))],
        out_specs=pl.BlockSpec((1, tq, 2 * v_dim), lambda bi, hp, qi: (bi, qi, hp)),
        compiler_params=_cparams("parallel", "parallel", "arbitrary"),
        name="attn_prompt",
    )(q, k, v)


def _paged_body(pt_ref, qa_ref, qr_ref, cn_ref, kn_ref, *rest, pg, n_heads, ts):
    del pt_ref
    lat_refs = rest[:pg]
    kr_refs = rest[pg:2 * pg]
    o_ref, m_ref, l_ref, acc_ref = rest[2 * pg:]
    j = pl.program_id(1)
    qa = qa_ref[0]
    qr = qr_ref[0]

    @pl.when(j == 0)
    def _():
        cn = cn_ref[0].astype(BF16)
        kn = kn_ref[0].astype(BF16)
        s = _dot_nt(qa, cn) + _dot_nt(qr, kn)
        r = lax.broadcasted_iota(jnp.int32, s.shape, 0) // n_heads
        c = lax.broadcasted_iota(jnp.int32, s.shape, 1)
        s = jnp.where(c <= r, s, NEG_BIG)
        m = jnp.max(s, axis=1, keepdims=True)
        p = jnp.exp(s - m)
        m_ref[...] = m
        l_ref[...] = jnp.sum(p, axis=1, keepdims=True)
        acc_ref[...] = _dot(p.astype(BF16), cn)

    lats = [r[...].astype(BF16) for r in lat_refs]
    s = jnp.concatenate([_dot_nt(qa, lats[i]) + _dot_nt(qr, kr_refs[i][...].astype(BF16))
                         for i in range(pg)], axis=1)
    m_old = m_ref[...]
    m_new = jnp.maximum(m_old, jnp.max(s, axis=1, keepdims=True))
    p32 = jnp.exp(s - m_new)
    p = p32.astype(BF16)
    alpha = jnp.exp(m_old - m_new)
    page = lats[0].shape[0]
    pv = _dot(p[:, :page], lats[0])
    for i in range(1, pg):
        pv = pv + _dot(p[:, i * page:(i + 1) * page], lats[i])
    l_ref[...] = alpha * l_ref[...] + jnp.sum(p32, axis=1, keepdims=True)
    acc_ref[...] = alpha * acc_ref[...] + pv
    m_ref[...] = m_new

    @pl.when(j == pl.num_programs(1) - 1)
    def _():
        o_ref[0] = (acc_ref[...] / l_ref[...]).astype(o_ref.dtype)


def _attn_paged(e, page_table, qa, qr, c_new, k_new, cache_latent, cache_krope, *, n_heads):
    db, rows, kv = qa.shape
    rope = qr.shape[2]
    n_pages = page_table.shape[1]
    page = cache_latent.shape[2]
    ts = rows // n_heads
    pg = 8 if n_pages % 8 == 0 else 1
    new_pad = c_new.shape[1]

    def page_map(i):
        return lambda b, j, pt: (e, pt[b * n_pages + j * pg + i], 0, 0)

    per_b = lambda b, j, pt: (b, 0, 0)
    grid_spec = pltpu.PrefetchScalarGridSpec(
        num_scalar_prefetch=1,
        grid=(db, n_pages // pg),
        in_specs=[pl.BlockSpec((1, rows, kv), per_b), pl.BlockSpec((1, rows, rope), per_b),
                  pl.BlockSpec((1, new_pad, kv), per_b), pl.BlockSpec((1, new_pad, rope), per_b)]
                 + [pl.BlockSpec((None, None, page, kv), page_map(i)) for i in range(pg)]
                 + [pl.BlockSpec((None, None, page, rope), page_map(i)) for i in range(pg)],
        out_specs=pl.BlockSpec((1, rows, kv), per_b),
        scratch_shapes=[pltpu.VMEM((rows, 1), F32), pltpu.VMEM((rows, 1), F32), pltpu.VMEM((rows, kv), F32)],
    )
    return pl.pallas_call(
        functools.partial(_paged_body, pg=pg, n_heads=n_heads, ts=ts),
        out_shape=jax.ShapeDtypeStruct((db, rows, kv), BF16),
        grid_spec=grid_spec,
        compiler_params=_cparams("parallel", "arbitrary"),
        name="attn_paged",
    )(page_table.reshape(-1), qa, qr, c_new, k_new, *([cache_latent] * pg), *([cache_krope] * pg))


def _uv_body(o_ref, w_ref, a_ref, *, n_heads):
    a_ref[...] = jnp.concatenate([_dot(o_ref[i], w_ref[i]) for i in range(n_heads)], axis=1).astype(a_ref.dtype)


def _uv_proj(o_hm, w_uv_h):
    n_heads, n, kv = o_hm.shape
    v_dim = w_uv_h.shape[2]
    return pl.pallas_call(
        functools.partial(_uv_body, n_heads=n_heads),
        out_shape=jax.ShapeDtypeStruct((n, n_heads * v_dim), BF16),
        grid=(1,),
        in_specs=[_resident(o_hm.shape), _resident(w_uv_h.shape)],
        out_specs=pl.BlockSpec((n, n_heads * v_dim), lambda i: (0, 0)),
        compiler_params=_cparams("arbitrary"),
        name="uv_proj",
    )(o_hm, w_uv_h)


def _conv_body(x_ref, att_ref, u_ref, hist_ref, cw_ref, cb_ref, lg_ref, lb_ref, woa_ref, woc_ref,
               o_ref, st_ref, win_ref, *, stride, width, sub):
    ti = pl.program_id(1)
    nt = pl.num_programs(1)
    tt = u_ref.shape[1]
    halo = hist_ref.shape[1]
    pad = halo // stride - (width - 1)

    @pl.when(ti == 0)
    def _():
        win_ref[0:halo, :] = hist_ref[0]

    win_ref[halo:halo + tt, :] = u_ref[0]

    cb = cb_ref[...]
    lg = lg_ref[...]
    lb = lb_ref[...]
    cvs = []
    for r0 in range(0, tt, sub):
        y = jnp.zeros((sub, u_ref.shape[2]), F32)
        for kk in range(width):
            y = y + cw_ref[kk:kk + 1, :] * win_ref[r0 + (kk + pad) * stride:r0 + (kk + pad) * stride + sub, :]
        y = y + cb
        yc = y - jnp.mean(y, axis=-1, keepdims=True)
        var = jnp.mean(yc * yc, axis=-1, keepdims=True)
        z = yc * lax.rsqrt(var + EPS) * lg + lb
        cvs.append((z * jax.nn.sigmoid(z)).astype(BF16))
    cv = jnp.concatenate(cvs, axis=0)
    o_ref[0] = x_ref[0] + _dot(att_ref[0], woa_ref[...]) + _dot(cv, woc_ref[...])

    @pl.when(ti == nt - 1)
    def _():
        st_ref[0] = win_ref[tt + pad * stride:tt + halo, :]

    if halo <= tt:
        @pl.when(ti < nt - 1)
        def _():
            win_ref[0:halo, :] = win_ref[tt:tt + halo, :]


def _conv_mix(x, att, u, hist, w, *, stride):
    bo, t, d = x.shape
    c = u.shape[2]
    a = att.shape[2]
    halo = hist.shape[1]
    width = w["conv_w"].shape[0]
    tt = min(512, t)
    assert t == tt or halo <= tt
    keep = (width - 1) * stride
    tile = lambda b, i: (b, i, 0)
    per_b = lambda b, i: (b, 0, 0)
    weights = [w["conv_w"], w["conv_b"], w["conv_ln_g"], w["conv_ln_b"], w["w_out_att"], w["w_out_conv"]]
    return pl.pallas_call(
        functools.partial(_conv_body, stride=stride, width=width, sub=min(64, tt)),
        out_shape=[jax.ShapeDtypeStruct((bo, t, d), F32), jax.ShapeDtypeStruct((bo, keep, c), F32)],
        grid=(bo, t // tt),
        in_specs=[pl.BlockSpec((1, tt, d), tile), pl.BlockSpec((1, tt, a), tile), pl.BlockSpec((1, tt, c), tile),
                  pl.BlockSpec((1, halo, c), per_b)] + [_resident(v.shape) for v in weights],
        out_specs=[pl.BlockSpec((1, tt, d), tile), pl.BlockSpec((1, keep, c), per_b)],
        scratch_shapes=[pltpu.VMEM((halo + tt, c), F32)],
        compiler_params=_cparams("parallel", "arbitrary"),
        name="conv_mix",
    )(x, att, u, hist, *weights)


def _pool_body(x_ref, hist_ref, g_ref, pw_ref, ps_ref, o_ref, st_ref, win_ref, *, stride, windows, pos0):
    ti = pl.program_id(1)
    nt = pl.num_programs(1)
    tt = x_ref.shape[1]
    halo = hist_ref.shape[1]
    d = x_ref.shape[2]
    grp = d // len(windows)

    @pl.when(ti == 0)
    def _():
        win_ref[0:halo, :] = hist_ref[0]

    x = x_ref[0]
    h = _rms(x, g_ref[...])
    win_ref[halo:halo + tt, :] = h

    pos = pos0 + ti * (tt // stride) + lax.broadcasted_iota(jnp.int32, (tt, 1), 0) // stride
    ys = []
    for gi, wn in enumerate(windows):
        cols = slice(gi * grp, (gi + 1) * grp)
        s = h[:, cols]
        for jj in range(1, wn):
            s = s + win_ref[halo - jj * stride:halo - jj * stride + tt, cols]
        cnt = jnp.minimum(pos + 1, wn).astype(F32)
        diff = (s / cnt - h[:, cols]).astype(BF16)
        ys.append(_dot(diff, pw_ref[gi]))
    o_ref[0] = x + jnp.concatenate(ys, axis=1) * ps_ref[...]

    keep = st_ref.shape[1]

    @pl.when(ti == nt - 1)
    def _():
        st_ref[0] = win_ref[tt + halo - keep:tt + halo, :]

    if halo <= tt:
        @pl.when(ti < nt - 1)
        def _():
            win_ref[0:halo, :] = win_ref[tt:tt + halo, :]


def _pool_mix(x, hist, g, pool_w, pool_scale, *, stride, windows, pos0):
    bo, t, d = x.shape
    halo = hist.shape[1]
    tt = min(512, t)
    assert t == tt or halo <= tt
    keep = (max(windows) - 1) * stride
    tile = lambda b, i: (b, i, 0)
    per_b = lambda b, i: (b, 0, 0)
    return pl.pallas_call(
        functools.partial(_pool_body, stride=stride, windows=windows, pos0=pos0),
        out_shape=[jax.ShapeDtypeStruct((bo, t, d), F32), jax.ShapeDtypeStruct((bo, keep, d), F32)],
        grid=(bo, t // tt),
        in_specs=[pl.BlockSpec((1, tt, d), tile), pl.BlockSpec((1, halo, d), per_b),
                  _resident(g.shape), _resident(pool_w.shape), _resident(pool_scale.shape)],
        out_specs=[pl.BlockSpec((1, tt, d), tile), pl.BlockSpec((1, keep, d), per_b)],
        scratch_shapes=[pltpu.VMEM((halo + tt, d), F32)],
        compiler_params=_cparams("parallel", "arbitrary"),
        name="pool_mix",
    )(x, hist, g, pool_w, pool_scale)


def _rope_tables(pos, nope, rope):
    inv = ROPE_BASE ** (-jnp.arange(0, rope, 2, dtype=F32) / rope)
    ang = pos[:, None] * inv[None, :]
    cos, sin = jnp.cos(ang), jnp.sin(ang)
    n = pos.shape[0]
    zero_hi = jnp.zeros((n, HEAD_PAD - nope - rope), F32)
    t1 = jnp.concatenate([jnp.ones((n, nope), F32), cos, cos, zero_hi], axis=1)
    t2 = jnp.concatenate([jnp.zeros((n, nope), F32), -sin, sin, zero_hi], axis=1)
    return t1, t2


def _time_major(a):
    b, s, c = a.shape
    return a.transpose(1, 0, 2).reshape(1, s * b, c)


def _batch_major(a, b):
    c = a.shape[2]
    return a.reshape(-1, b, c).transpose(1, 0, 2)


def kernel(x_prompt, x_sample, cache_latent, cache_krope, state_conv, state_pool, page_table, ffn1_norm, ffn1_w_gate, ffn1_w_up, ffn1_w_down, mix_norm, w_in, q_norm, w_uq, kv_norm, w_uk, w_uv, conv_w, conv_b, conv_ln_g, conv_ln_b, w_out, pool_w, pool_scale, ffn2_norm, ffn2_w_gate, ffn2_w_up, ffn2_w_down, final_norm):
    B, T, D = x_prompt.shape
    DB, TS, _ = x_sample.shape
    depth = ffn1_norm.shape[0]
    q_lora, n_heads, qk_dim = w_uq.shape[1:]
    kv_lora, _, nope = w_uk.shape[1:]
    v_dim = w_uv.shape[3]
    rope = qk_dim - nope
    conv_ch = conv_w.shape[2]
    conv_width = conv_w.shape[1]
    n_groups = pool_w.shape[1]
    grp = D // n_groups
    windows = tuple(2 ** (i + 1) for i in range(n_groups))
    pool_state = state_pool.shape[2]
    n_pages = page_table.shape[1]
    page = cache_latent.shape[2]
    past_len = n_pages * page
    scale = float(qk_dim) ** -0.5
    assert nope == 64 and rope == 32 and v_dim == 64 and max(windows) - 1 == pool_state and grp * n_groups == D

    t1_p, t2_p = _rope_tables(jnp.arange(T, dtype=F32), nope, rope)
    t1_p, t2_p = jnp.tile(t1_p, (B, 1)), jnp.tile(t2_p, (B, 1))
    t1_s, t2_s = _rope_tables(past_len + jnp.arange(TS, dtype=F32), nope, rope)
    t1_s, t2_s = jnp.tile(t1_s, (DB, 1)), jnp.tile(t2_s, (DB, 1))

    def pad_rope_cols(wr, lead):
        z0 = jnp.zeros(wr.shape[:-1] + (lead,), wr.dtype)
        z1 = jnp.zeros(wr.shape[:-1] + (HEAD_PAD - lead - rope,), wr.dtype)
        return jnp.concatenate([z0, wr, z1], axis=-1)

    def even_weights(l):
        e = l // 2
        wi = w_in[e]
        o = 0
        w_q = wi[:, o:o + q_lora]; o += q_lora
        w_kv = wi[:, o:o + kv_lora]; o += kv_lora
        w_kr = wi[:, o:o + rope]; o += rope
        w_a = wi[:, o:o + conv_ch]; o += conv_ch
        w_g = wi[:, o:o + conv_ch]
        uq = w_uq[e] * scale
        uq = jnp.concatenate([uq, jnp.zeros((q_lora, n_heads, HEAD_PAD - qk_dim), F32)], axis=-1)
        uk = w_uk[e]
        uk_pad = jnp.concatenate([uk, jnp.zeros((kv_lora, n_heads, HEAD_PAD - nope), F32)], axis=-1)
        uk_abs = jnp.concatenate([uk.transpose(1, 2, 0),
                                  jnp.zeros((n_heads, HEAD_PAD - nope, kv_lora), F32)], axis=1)
        return dict(
            n_heads=n_heads, rope=rope,
            mix_norm=mix_norm[l][None], w_q=w_q.astype(BF16), w_kv=w_kv.astype(BF16),
            w_kr=pad_rope_cols(w_kr, nope).astype(BF16), w_a=w_a.astype(BF16), w_g=w_g.astype(BF16),
            q_norm=q_norm[e][None], kv_norm=kv_norm[e][None],
            w_uq=uq.reshape(q_lora, n_heads * HEAD_PAD).astype(BF16),
            w_uk_pad=uk_pad.reshape(kv_lora, n_heads * HEAD_PAD).astype(BF16),
            w_uk_abs=uk_abs.astype(BF16),
            w_uv=w_uv[e].reshape(kv_lora, n_heads * v_dim).astype(BF16),
            w_uv_h=w_uv[e].transpose(1, 0, 2).astype(BF16),
            conv_w=conv_w[e], conv_b=conv_b[e][None], conv_ln_g=conv_ln_g[e][None], conv_ln_b=conv_ln_b[e][None],
            w_out_att=w_out[e][:n_heads * v_dim].astype(BF16), w_out_conv=w_out[e][n_heads * v_dim:].astype(BF16),
        )

    xp = x_prompt.reshape(B * T, D)
    xs = x_sample.reshape(DB * TS, D)
    lat_p, kr_p, lat_s, kr_s, conv_p, conv_s, pool_p, pool_s = ([] for _ in range(8))
    conv_halo = 32 if conv_width - 1 <= 32 else conv_width - 1
    pool_halo = pool_state + 1

    for l in range(depth):
        fg = final_norm[None]
        w1 = (ffn1_norm[l][None], ffn1_w_gate[l].astype(BF16), ffn1_w_up[l].astype(BF16), ffn1_w_down[l].astype(BF16))
        xp = _ffn(xp, *w1, fg)
        xs = _ffn(xs, *w1, fg)
        if l % 2 == 0:
            e = l // 2
            w = even_weights(l)
            q_p, c_p, k_rope_p, u_p, k_p, v_p = _proj(xp, t1_p, t2_p, w, absorbed=False)
            q_s, c_s, k_rope_s, u_s, qa_s = _proj(xs, t1_s, t2_s, w, absorbed=True)

            att_p = _attn_prompt(q_p.reshape(B, T, -1), k_p.reshape(B, T, -1), v_p.reshape(B, T, -1),
                                 n_heads=n_heads, v_dim=v_dim)

            qa = qa_s.reshape(DB, TS * n_heads, kv_lora)
            qr = q_s.reshape(DB, TS * n_heads, HEAD_PAD)[:, :, nope:nope + rope]
            new_pad = 16
            c_new = jnp.pad(c_s.reshape(DB, TS, kv_lora), ((0, 0), (0, new_pad - TS), (0, 0)))
            k_new = jnp.pad(k_rope_s.reshape(DB, TS, rope), ((0, 0), (0, new_pad - TS), (0, 0)))
            o_s = _attn_paged(e, page_table, qa, qr, c_new, k_new, cache_latent, cache_krope, n_heads=n_heads)
            o_hm = o_s.reshape(DB * TS, n_heads, kv_lora).transpose(1, 0, 2)
            att_s = _uv_proj(o_hm, w["w_uv_h"])

            xp3, st_p = _conv_mix(xp.reshape(B, T, D), att_p, u_p.reshape(B, T, conv_ch),
                                  jnp.zeros((B, conv_halo, conv_ch), F32), w, stride=1)
            xp = xp3.reshape(B * T, D)
            hist_s = jnp.pad(state_conv[e], ((0, 0), (conv_halo - (conv_width - 1), 0), (0, 0)))
            xs3, st_s = _conv_mix(_time_major(xs.reshape(DB, TS, D)), _time_major(att_s.reshape(DB, TS, -1)),
                                  _time_major(u_s.reshape(DB, TS, conv_ch)), _time_major(hist_s), w, stride=DB)
            xs = _batch_major(xs3, DB).reshape(DB * TS, D)

            lat_p.append(c_p.reshape(B, T, kv_lora)); kr_p.append(k_rope_p.reshape(B, T, rope))
            lat_s.append(c_s.reshape(DB, TS, kv_lora)); kr_s.append(k_rope_s.reshape(DB, TS, rope))
            conv_p.append(st_p); conv_s.append(_batch_major(st_s, DB))
        else:
            o = l // 2
            g = mix_norm[l][None]
            pw = pool_w[o].astype(BF16)
            ps = pool_scale[o][None]
            xp3, st_p = _pool_mix(xp.reshape(B, T, D), jnp.zeros((B, pool_halo, D), F32), g, pw, ps,
                                  stride=1, windows=windows, pos0=0)
            xp = xp3.reshape(B * T, D)
            hist_s = jnp.pad(state_pool[o], ((0, 0), (pool_halo - pool_state, 0), (0, 0)))
            xs3, st_s = _pool_mix(_time_major(xs.reshape(DB, TS, D)), _time_major(hist_s), g, pw, ps,
                                  stride=DB, windows=windows, pos0=past_len)
            xs = _batch_major(xs3, DB).reshape(DB * TS, D)
            pool_p.append(st_p); pool_s.append(_batch_major(st_s, DB))
        w2 = (ffn2_norm[l][None], ffn2_w_gate[l].astype(BF16), ffn2_w_up[l].astype(BF16), ffn2_w_down[l].astype(BF16))
        last = l == depth - 1
        xp = _ffn(xp, *w2, fg, final=last)
        xs = _ffn(xs, *w2, fg, final=last)

    return (xp.reshape(B, T, D), xs.reshape(DB, TS, D),
            jnp.stack(lat_p), jnp.stack(kr_p), jnp.stack(lat_s), jnp.stack(kr_s),
            jnp.stack(conv_p), jnp.stack(conv_s), jnp.stack(pool_p), jnp.stack(pool_s))
```

```python
import functools

import jax
import jax.numpy as jnp
from jax import lax
from jax.experimental import pallas as pl
from jax.experimental.pallas import tpu as pltpu

EPS = 1e-6
ROPE_BASE = 10000.0
LANES = 128
SUBLANES = 8
HEAD_PAD = 128
NEG_BIG = -1e30
VMEM_LIMIT = 56 * 1024 * 1024
BF16 = jnp.bfloat16
F32 = jnp.float32


def _cparams(*sem):
    return pltpu.CompilerParams(dimension_semantics=sem, vmem_limit_bytes=VMEM_LIMIT)


def _resident(shape):
    nd = len(shape)
    return pl.BlockSpec(shape, lambda *_: (0,) * nd, pipeline_mode=pl.Buffered(1))


def _rms(x, g):
    return x * lax.rsqrt(jnp.mean(x * x, axis=-1, keepdims=True) + EPS) * g


def _dot(a, b):
    return jnp.dot(a, b, preferred_element_type=F32)


def _dot_nt(a, b):
    return lax.dot_general(a, b, (((1,), (1,)), ((), ())), preferred_element_type=F32)


def _ffn_body(x_ref, g_ref, wg_ref, wu_ref, wd_ref, fg_ref, o_ref, *, chunk, final):
    x = x_ref[...]
    h = _rms(x, g_ref[...]).astype(BF16)
    d_ff = wg_ref.shape[1]
    acc = jnp.zeros(x.shape, F32)
    for c in range(d_ff // chunk):
        sl = slice(c * chunk, (c + 1) * chunk)
        gate = _dot(h, wg_ref[:, sl].astype(BF16))
        up = _dot(h, wu_ref[:, sl].astype(BF16))
        act = (gate * jax.nn.sigmoid(gate) * up).astype(BF16)
        acc = acc + _dot(act, wd_ref[sl, :].astype(BF16))
    y = x + 0.5 * acc
    if final:
        y = _rms(y, fg_ref[...])
    o_ref[...] = y


def _ffn(x, g, wg, wu, wd, fg, layer, *, final=False):
    n, d = x.shape
    d_ff = wg.shape[2]
    tm = min(512, n)
    chunk = 256 if d_ff % 256 == 0 else LANES
    row = lambda i: (i, 0)
    layer_block = lambda shape: pl.BlockSpec((None,) + shape, lambda i: (layer, 0, 0), pipeline_mode=pl.Buffered(1))
    return pl.pallas_call(
        functools.partial(_ffn_body, chunk=chunk, final=final),
        out_shape=jax.ShapeDtypeStruct((n, d), F32),
        grid=(n // tm,),
        in_specs=[pl.BlockSpec((tm, d), row), layer_block((1, d)), layer_block((d, d_ff)),
                  layer_block((d, d_ff)), layer_block((d_ff, d)), _resident((1, d))],
        out_specs=pl.BlockSpec((tm, d), row),
        compiler_params=_cparams("parallel"),
        name="ffn",
    )(x, g.reshape(g.shape[0], 1, d), wg, wu, wd, fg)


def _rope(v, t1, t2):
    lane = lax.broadcasted_iota(jnp.int32, v.shape, 1)
    partner = jnp.where(lane < 80, pltpu.roll(v, 112, 1), pltpu.roll(v, 16, 1))
    return v * t1 + partner * t2


def _proj_body(x_ref, t1_ref, t2_ref, g_ref, wq_ref, wkv_ref, wkr_ref, wa_ref, wgl_ref, qn_ref, kvn_ref,
               wuq_ref, wk_ref, *rest, absorbed, n_heads):
    if absorbed:
        q_ref, lat_ref, kr_ref, u_ref, qa_ref = rest
    else:
        wv_ref, q_ref, lat_ref, kr_ref, u_ref, k_ref, v_ref = rest
    x = x_ref[...]
    h = _rms(x, g_ref[...]).astype(BF16)
    t1 = t1_ref[...]
    t2 = t2_ref[...]

    u_ref[...] = _dot(h, wa_ref[...]) * jax.nn.sigmoid(_dot(h, wgl_ref[...]))

    c_kv = _rms(_dot(h, wkv_ref[...]), kvn_ref[...])
    lat_ref[...] = c_kv
    c_bf = c_kv.astype(BF16)

    kr = _rope(_dot(h, wkr_ref[...]), t1, t2)
    kr_ref[...] = kr[:, 64:96]

    qn = _rms(_dot(h, wq_ref[...]), qn_ref[...]).astype(BF16)
    q = _dot(qn, wuq_ref[...])
    q_heads = [_rope(q[:, i * HEAD_PAD:(i + 1) * HEAD_PAD], t1, t2).astype(BF16) for i in range(n_heads)]
    q_ref[...] = jnp.concatenate(q_heads, axis=1)

    if absorbed:
        kv = wk_ref.shape[2]
        for i in range(n_heads):
            qa_ref[:, i * kv:(i + 1) * kv] = _dot(q_heads[i], wk_ref[i]).astype(BF16)
    else:
        k = _dot(c_bf, wk_ref[...])
        k_ref[...] = jnp.concatenate(
            [(k[:, i * HEAD_PAD:(i + 1) * HEAD_PAD] + kr).astype(BF16) for i in range(n_heads)], axis=1)
        v_ref[...] = _dot(c_bf, wv_ref[...]).astype(BF16)


def _proj(x, t1, t2, w, *, absorbed):
    n, d = x.shape
    tm = min(512, n)
    n_heads = w["n_heads"]
    kv = w["w_kv"].shape[1]
    rope = w["rope"]
    conv_ch = w["w_a"].shape[1]
    row = lambda i: (i, 0)
    weights = [w["mix_norm"], w["w_q"], w["w_kv"], w["w_kr"], w["w_a"], w["w_g"], w["q_norm"], w["kv_norm"],
               w["w_uq"]] + ([w["w_uk_abs"]] if absorbed else [w["w_uk_pad"], w["w_uv"]])
    extra_w = [n_heads * kv] if absorbed else [n_heads * HEAD_PAD, w["w_uv"].shape[1]]
    widths = [(n_heads * HEAD_PAD, BF16), (kv, F32), (rope, F32), (conv_ch, F32)] + [(c, BF16) for c in extra_w]
    return pl.pallas_call(
        functools.partial(_proj_body, absorbed=absorbed, n_heads=n_heads),
        out_shape=[jax.ShapeDtypeStruct((n, c), dt) for c, dt in widths],
        grid=(n // tm,),
        in_specs=[pl.BlockSpec((tm, d), row), pl.BlockSpec((tm, LANES), row), pl.BlockSpec((tm, LANES), row)]
                 + [_resident(a.shape) for a in weights],
        out_specs=[pl.BlockSpec((tm, c), row) for c, _ in widths],
        compiler_params=_cparams("parallel"),
        name="proj_abs" if absorbed else "proj",
    )(x, t1, t2, *weights)


def _attn_body(q_ref, k_ref, v_ref, o_ref, *, tq, v_dim):
    qi = pl.program_id(2)
    outs = []
    for hh in range(2):
        cols = slice(hh * HEAD_PAD, (hh + 1) * HEAD_PAD)
        q = q_ref[0, :, cols]

        def step(j, carry, masked):
            m, l, acc = carry
            k = k_ref[0, pl.ds(pl.multiple_of(j * tq, tq), tq), cols]
            v = v_ref[0, pl.ds(pl.multiple_of(j * tq, tq), tq), :]
            s = _dot_nt(q, k)
            if masked:
                r = lax.broadcasted_iota(jnp.int32, s.shape, 0)
                c = lax.broadcasted_iota(jnp.int32, s.shape, 1)
                s = jnp.where(c <= r, s, NEG_BIG)
            m_new = jnp.maximum(m, jnp.max(s, axis=1, keepdims=True))
            p = jnp.exp(s - m_new)
            alpha = jnp.exp(m - m_new)
            l = alpha * l + jnp.sum(p, axis=1, keepdims=True)
            acc = alpha * acc + _dot(p.astype(BF16), v)
            return m_new, l, acc

        init = (jnp.full((tq, 1), NEG_BIG, F32), jnp.zeros((tq, 1), F32), jnp.zeros((tq, 2 * v_dim), F32))
        carry = lax.fori_loop(0, qi, functools.partial(step, masked=False), init)
        m, l, acc = step(qi, carry, True)
        outs.append(acc / l)
    lane = lax.broadcasted_iota(jnp.int32, outs[0].shape, 1)
    o_ref[0] = jnp.where(lane < v_dim, outs[0], outs[1]).astype(o_ref.dtype)


def _attn_prompt(q, k, v, *, n_heads, v_dim):
    b, t, _ = q.shape
    tq = min(512, t)
    return pl.pallas_call(
        functools.partial(_attn_body, tq=tq, v_dim=v_dim),
        out_shape=jax.ShapeDtypeStruct((b, t, n_heads * v_dim), BF16),
        grid=(b, n_heads // 2, t // tq),
        in_specs=[pl.BlockSpec((1, tq, 2 * HEAD_PAD), lambda bi, hp, qi: (bi, qi, hp)),
                  pl.BlockSpec((1, t, 2 * HEAD_PAD), lambda bi, hp, qi: (bi, 0, hp)),
                  pl.BlockSpec((1, t, 2 * v_dim), lambda bi, hp, qi: (bi, 0, hp))],
        out_specs=pl.BlockSpec((1, tq, 2 * v_dim), lambda bi, hp, qi: (bi, qi, hp)),
        compiler_params=_cparams("parallel", "parallel", "arbitrary"),
        name="attn_prompt",
    )(q, k, v)


def _paged_body(pt_ref, qa_ref, qr_ref, cn_ref, kn_ref, lat_hbm, krt_hbm, o_ref, lat_buf, krt_buf, sems,
                *, e, n_pages, n_heads):
    b = pl.program_id(0)
    page = krt_hbm.shape[3]

    def page_copies(seq, slot):
        copies = []
        for pi in range(n_pages):
            pid = pt_ref[seq * n_pages + pi]
            rows = pl.ds(pi * page, page)
            copies.append(pltpu.make_async_copy(lat_hbm.at[e, pid], lat_buf.at[slot, rows], sems.at[0, slot]))
            copies.append(pltpu.make_async_copy(krt_hbm.at[e, pid], krt_buf.at[slot, :, rows], sems.at[1, slot]))
        return copies

    @pl.when(b == 0)
    def _():
        for cp in page_copies(0, 0):
            cp.start()

    @pl.when(b + 1 < pl.num_programs(0))
    def _():
        for cp in page_copies(b + 1, (b + 1) % 2):
            cp.start()

    slot = b % 2
    for cp in page_copies(b, slot):
        cp.wait()

    qa = qa_ref[0]
    qr = qr_ref[0]
    lat = lat_buf[slot].astype(BF16)
    krt = krt_buf[slot].astype(BF16)
    cn = cn_ref[0].astype(BF16)
    kn = kn_ref[0].astype(BF16)
    s = _dot_nt(qa, lat) + _dot(qr, krt)
    sn = _dot_nt(qa, cn) + _dot_nt(qr, kn)
    r = lax.broadcasted_iota(jnp.int32, sn.shape, 0) // n_heads
    c = lax.broadcasted_iota(jnp.int32, sn.shape, 1)
    sn = jnp.where(c <= r, sn, NEG_BIG)
    m = jnp.maximum(jnp.max(s, axis=1, keepdims=True), jnp.max(sn, axis=1, keepdims=True))
    p = jnp.exp(s - m)
    pn = jnp.exp(sn - m)
    l = jnp.sum(p, axis=1, keepdims=True) + jnp.sum(pn, axis=1, keepdims=True)
    o = _dot(p.astype(BF16), lat) + _dot(pn.astype(BF16), cn)
    o_ref[0] = (o / l).astype(o_ref.dtype)


def _attn_paged(e, page_table, qa, qr, c_new, k_new, cache_latent, cache_krope_t, *, n_heads):
    db, rows, kv = qa.shape
    rope = qr.shape[2]
    n_pages = page_table.shape[1]
    page = cache_latent.shape[2]
    past = n_pages * page
    new_pad = c_new.shape[1]
    per_b = lambda b, pt: (b, 0, 0)
    grid_spec = pltpu.PrefetchScalarGridSpec(
        num_scalar_prefetch=1,
        grid=(db,),
        in_specs=[pl.BlockSpec((1, rows, kv), per_b), pl.BlockSpec((1, rows, rope), per_b),
                  pl.BlockSpec((1, new_pad, kv), per_b), pl.BlockSpec((1, new_pad, rope), per_b),
                  pl.BlockSpec(memory_space=pl.ANY), pl.BlockSpec(memory_space=pl.ANY)],
        out_specs=pl.BlockSpec((1, rows, kv), per_b),
        scratch_shapes=[pltpu.VMEM((2, past, kv), F32), pltpu.VMEM((2, rope, past), F32),
                        pltpu.SemaphoreType.DMA((2, 2))],
    )
    return pl.pallas_call(
        functools.partial(_paged_body, e=e, n_pages=n_pages, n_heads=n_heads),
        out_shape=jax.ShapeDtypeStruct((db, rows, kv), BF16),
        grid_spec=grid_spec,
        compiler_params=_cparams("arbitrary"),
        name="attn_paged",
    )(page_table.reshape(-1), qa, qr, c_new, k_new, cache_latent, cache_krope_t)


def _uv_body(o_ref, w_ref, a_ref, *, n_heads):
    a_ref[...] = jnp.concatenate([_dot(o_ref[i], w_ref[i]) for i in range(n_heads)], axis=1).astype(a_ref.dtype)


def _uv_proj(o_hm, w_uv_h):
    n_heads, n, kv = o_hm.shape
    v_dim = w_uv_h.shape[2]
    return pl.pallas_call(
        functools.partial(_uv_body, n_heads=n_heads),
        out_shape=jax.ShapeDtypeStruct((n, n_heads * v_dim), BF16),
        grid=(1,),
        in_specs=[_resident(o_hm.shape), _resident(w_uv_h.shape)],
        out_specs=pl.BlockSpec((n, n_heads * v_dim), lambda i: (0, 0)),
        compiler_params=_cparams("arbitrary"),
        name="uv_proj",
    )(o_hm, w_uv_h)


def _conv_body(x_ref, att_ref, u_ref, hist_ref, cw_ref, cb_ref, lg_ref, lb_ref, woa_ref, woc_ref,
               o_ref, st_ref, win_ref, *shift_refs, stride, width, sub):
    ti = pl.program_id(1)
    nt = pl.num_programs(1)
    tt = u_ref.shape[1]
    halo = hist_ref.shape[1]
    pad = halo // stride - (width - 1)

    @pl.when(ti == 0)
    def _():
        win_ref[0:halo, :] = hist_ref[0]

    win_ref[halo:halo + tt, :] = u_ref[0]

    offsets = [(kk + pad) * stride for kk in range(width)]
    shifts = _conv_shifts(offsets)
    for sh, ref in zip(shifts, shift_refs):
        ref[...] = win_ref[sh:sh + ref.shape[0], :]

    def tap(off, r0):
        sh = off % SUBLANES
        src = win_ref if sh == 0 else shift_refs[shifts.index(sh)]
        return src[r0 + off - sh:r0 + off - sh + sub, :]

    cb = cb_ref[...]
    lg = lg_ref[...]
    lb = lb_ref[...]
    cvs = []
    for r0 in range(0, tt, sub):
        y = jnp.zeros((sub, u_ref.shape[2]), F32)
        for kk in range(width):
            y = y + cw_ref[kk:kk + 1, :] * tap(offsets[kk], r0)
        y = y + cb
        yc = y - jnp.mean(y, axis=-1, keepdims=True)
        var = jnp.mean(yc * yc, axis=-1, keepdims=True)
        z = yc * lax.rsqrt(var + EPS) * lg + lb
        cvs.append((z * jax.nn.sigmoid(z)).astype(BF16))
    cv = jnp.concatenate(cvs, axis=0)
    o_ref[0] = x_ref[0] + _dot(att_ref[0], woa_ref[...]) + _dot(cv, woc_ref[...])

    @pl.when(ti == nt - 1)
    def _():
        st_ref[0] = win_ref[tt + pad * stride:tt + halo, :]

    if halo <= tt:
        @pl.when(ti < nt - 1)
        def _():
            win_ref[0:halo, :] = win_ref[tt:tt + halo, :]


def _conv_shifts(offsets):
    return sorted({off % SUBLANES for off in offsets} - {0})


def _conv_mix(x, att, u, hist, w, *, stride):
    bo, t, d = x.shape
    c = u.shape[2]
    a = att.shape[2]
    halo = hist.shape[1]
    width = w["conv_w"].shape[0]
    tt = min(512, t)
    assert t == tt or halo <= tt
    keep = (width - 1) * stride
    pad = halo // stride - (width - 1)
    n_shift = len(_conv_shifts([(kk + pad) * stride for kk in range(width)]))
    tile = lambda b, i: (b, i, 0)
    per_b = lambda b, i: (b, 0, 0)
    weights = [w["conv_w"], w["conv_b"], w["conv_ln_g"], w["conv_ln_b"], w["w_out_att"], w["w_out_conv"]]
    return pl.pallas_call(
        functools.partial(_conv_body, stride=stride, width=width, sub=min(64, tt)),
        out_shape=[jax.ShapeDtypeStruct((bo, t, d), F32), jax.ShapeDtypeStruct((bo, keep, c), F32)],
        grid=(bo, t // tt),
        in_specs=[pl.BlockSpec((1, tt, d), tile), pl.BlockSpec((1, tt, a), tile), pl.BlockSpec((1, tt, c), tile),
                  pl.BlockSpec((1, halo, c), per_b)] + [_resident(v.shape) for v in weights],
        out_specs=[pl.BlockSpec((1, tt, d), tile), pl.BlockSpec((1, keep, c), per_b)],
        scratch_shapes=[pltpu.VMEM((halo + tt, c), F32)] + [pltpu.VMEM((halo + tt - SUBLANES, c), F32)] * n_shift,
        compiler_params=_cparams("parallel", "arbitrary"),
        name="conv_mix",
    )(x, att, u, hist, *weights)


def _pool_body(x_ref, hist_ref, g_ref, pw_ref, ps_ref, o_ref, st_ref, win_ref, *, stride, windows, pos0):
    ti = pl.program_id(1)
    nt = pl.num_programs(1)
    tt = x_ref.shape[1]
    halo = hist_ref.shape[1]
    d = x_ref.shape[2]
    grp = d // len(windows)

    @pl.when(ti == 0)
    def _():
        win_ref[0:halo, :] = hist_ref[0]

    x = x_ref[0]
    h = _rms(x, g_ref[...])
    win_ref[halo:halo + tt, :] = h

    pos = pos0 + ti * (tt // stride) + lax.broadcasted_iota(jnp.int32, (tt, 1), 0) // stride
    ys = []
    for gi, wn in enumerate(windows):
        cols = slice(gi * grp, (gi + 1) * grp)
        s = h[:, cols]
        for jj in range(1, wn):
            s = s + win_ref[halo - jj * stride:halo - jj * stride + tt, cols]
        cnt = jnp.minimum(pos + 1, wn).astype(F32)
        diff = (s / cnt - h[:, cols]).astype(BF16)
        ys.append(_dot(diff, pw_ref[gi]))
    o_ref[0] = x + jnp.concatenate(ys, axis=1) * ps_ref[...]

    keep = st_ref.shape[1]

    @pl.when(ti == nt - 1)
    def _():
        st_ref[0] = win_ref[tt + halo - keep:tt + halo, :]

    if halo <= tt:
        @pl.when(ti < nt - 1)
        def _():
            win_ref[0:halo, :] = win_ref[tt:tt + halo, :]


def _pool_mix(x, hist, g, pool_w, pool_scale, *, stride, windows, pos0):
    bo, t, d = x.shape
    halo = hist.shape[1]
    tt = min(512, t)
    assert t == tt or halo <= tt
    keep = (max(windows) - 1) * stride
    tile = lambda b, i: (b, i, 0)
    per_b = lambda b, i: (b, 0, 0)
    return pl.pallas_call(
        functools.partial(_pool_body, stride=stride, windows=windows, pos0=pos0),
        out_shape=[jax.ShapeDtypeStruct((bo, t, d), F32), jax.ShapeDtypeStruct((bo, keep, d), F32)],
        grid=(bo, t // tt),
        in_specs=[pl.BlockSpec((1, tt, d), tile), pl.BlockSpec((1, halo, d), per_b),
                  _resident(g.shape), _resident(pool_w.shape), _resident(pool_scale.shape)],
        out_specs=[pl.BlockSpec((1, tt, d), tile), pl.BlockSpec((1, keep, d), per_b)],
        scratch_shapes=[pltpu.VMEM((halo + tt, d), F32)],
        compiler_params=_cparams("parallel", "arbitrary"),
        name="pool_mix",
    )(x, hist, g, pool_w, pool_scale)


def _rope_tables(pos, nope, rope):
    inv = ROPE_BASE ** (-jnp.arange(0, rope, 2, dtype=F32) / rope)
    ang = pos[:, None] * inv[None, :]
    cos, sin = jnp.cos(ang), jnp.sin(ang)
    n = pos.shape[0]
    zero_hi = jnp.zeros((n, HEAD_PAD - nope - rope), F32)
    t1 = jnp.concatenate([jnp.ones((n, nope), F32), cos, cos, zero_hi], axis=1)
    t2 = jnp.concatenate([jnp.zeros((n, nope), F32), -sin, sin, zero_hi], axis=1)
    return t1, t2


def _time_major(a):
    b, s, c = a.shape
    return a.transpose(1, 0, 2).reshape(1, s * b, c)


def _batch_major(a, b):
    c = a.shape[2]
    return a.reshape(-1, b, c).transpose(1, 0, 2)


def kernel(x_prompt, x_sample, cache_latent, cache_krope, state_conv, state_pool, page_table, ffn1_norm, ffn1_w_gate, ffn1_w_up, ffn1_w_down, mix_norm, w_in, q_norm, w_uq, kv_norm, w_uk, w_uv, conv_w, conv_b, conv_ln_g, conv_ln_b, w_out, pool_w, pool_scale, ffn2_norm, ffn2_w_gate, ffn2_w_up, ffn2_w_down, final_norm):
    B, T, D = x_prompt.shape
    DB, TS, _ = x_sample.shape
    depth = ffn1_norm.shape[0]
    q_lora, n_heads, qk_dim = w_uq.shape[1:]
    kv_lora, _, nope = w_uk.shape[1:]
    v_dim = w_uv.shape[3]
    rope = qk_dim - nope
    conv_ch = conv_w.shape[2]
    conv_width = conv_w.shape[1]
    n_groups = pool_w.shape[1]
    grp = D // n_groups
    windows = tuple(2 ** (i + 1) for i in range(n_groups))
    pool_state = state_pool.shape[2]
    n_pages = page_table.shape[1]
    page = cache_latent.shape[2]
    past_len = n_pages * page
    scale = float(qk_dim) ** -0.5
    assert nope == 64 and rope == 32 and v_dim == 64 and max(windows) - 1 == pool_state and grp * n_groups == D

    t1_p, t2_p = _rope_tables(jnp.arange(T, dtype=F32), nope, rope)
    t1_p, t2_p = jnp.tile(t1_p, (B, 1)), jnp.tile(t2_p, (B, 1))
    t1_s, t2_s = _rope_tables(past_len + jnp.arange(TS, dtype=F32), nope, rope)
    t1_s, t2_s = jnp.tile(t1_s, (DB, 1)), jnp.tile(t2_s, (DB, 1))

    def pad_rope_cols(wr, lead):
        z0 = jnp.zeros(wr.shape[:-1] + (lead,), wr.dtype)
        z1 = jnp.zeros(wr.shape[:-1] + (HEAD_PAD - lead - rope,), wr.dtype)
        return jnp.concatenate([z0, wr, z1], axis=-1)

    def even_weights(l):
        e = l // 2
        wi = w_in[e]
        o = 0
        w_q = wi[:, o:o + q_lora]; o += q_lora
        w_kv = wi[:, o:o + kv_lora]; o += kv_lora
        w_kr = wi[:, o:o + rope]; o += rope
        w_a = wi[:, o:o + conv_ch]; o += conv_ch
        w_g = wi[:, o:o + conv_ch]
        uq = w_uq[e] * scale
        uq = jnp.concatenate([uq, jnp.zeros((q_lora, n_heads, HEAD_PAD - qk_dim), F32)], axis=-1)
        uk = w_uk[e]
        uk_pad = jnp.concatenate([uk, jnp.zeros((kv_lora, n_heads, HEAD_PAD - nope), F32)], axis=-1)
        uk_abs = jnp.concatenate([uk.transpose(1, 2, 0),
                                  jnp.zeros((n_heads, HEAD_PAD - nope, kv_lora), F32)], axis=1)
        return dict(
            n_heads=n_heads, rope=rope,
            mix_norm=mix_norm[l][None], w_q=w_q.astype(BF16), w_kv=w_kv.astype(BF16),
            w_kr=pad_rope_cols(w_kr, nope).astype(BF16), w_a=w_a.astype(BF16), w_g=w_g.astype(BF16),
            q_norm=q_norm[e][None], kv_norm=kv_norm[e][None],
            w_uq=uq.reshape(q_lora, n_heads * HEAD_PAD).astype(BF16),
            w_uk_pad=uk_pad.reshape(kv_lora, n_heads * HEAD_PAD).astype(BF16),
            w_uk_abs=uk_abs.astype(BF16),
            w_uv=w_uv[e].reshape(kv_lora, n_heads * v_dim).astype(BF16),
            w_uv_h=w_uv[e].transpose(1, 0, 2).astype(BF16),
            conv_w=conv_w[e], conv_b=conv_b[e][None], conv_ln_g=conv_ln_g[e][None], conv_ln_b=conv_ln_b[e][None],
            w_out_att=w_out[e][:n_heads * v_dim].astype(BF16), w_out_conv=w_out[e][n_heads * v_dim:].astype(BF16),
        )

    cache_krope_t = jnp.swapaxes(cache_krope, 2, 3)
    xp = x_prompt.reshape(B * T, D)
    xs = x_sample.reshape(DB * TS, D)
    lat_p, kr_p, lat_s, kr_s, conv_p, conv_s, pool_p, pool_s = ([] for _ in range(8))
    conv_halo = 32 if conv_width - 1 <= 32 else conv_width - 1
    pool_halo = pool_state + 1

    for l in range(depth):
        fg = final_norm[None]
        w1 = (ffn1_norm, ffn1_w_gate, ffn1_w_up, ffn1_w_down)
        xp = _ffn(xp, *w1, fg, l)
        xs = _ffn(xs, *w1, fg, l)
        if l % 2 == 0:
            e = l // 2
            w = even_weights(l)
            q_p, c_p, k_rope_p, u_p, k_p, v_p = _proj(xp, t1_p, t2_p, w, absorbed=False)
            q_s, c_s, k_rope_s, u_s, qa_s = _proj(xs, t1_s, t2_s, w, absorbed=True)

            att_p = _attn_prompt(q_p.reshape(B, T, -1), k_p.reshape(B, T, -1), v_p.reshape(B, T, -1),
                                 n_heads=n_heads, v_dim=v_dim)

            qa = qa_s.reshape(DB, TS * n_heads, kv_lora)
            qr = q_s.reshape(DB, TS * n_heads, HEAD_PAD)[:, :, nope:nope + rope]
            new_pad = 16
            c_new = jnp.pad(c_s.reshape(DB, TS, kv_lora), ((0, 0), (0, new_pad - TS), (0, 0)))
            k_new = jnp.pad(k_rope_s.reshape(DB, TS, rope), ((0, 0), (0, new_pad - TS), (0, 0)))
            o_s = _attn_paged(e, page_table, qa, qr, c_new, k_new, cache_latent, cache_krope_t, n_heads=n_heads)
            o_hm = o_s.reshape(DB * TS, n_heads, kv_lora).transpose(1, 0, 2)
            att_s = _uv_proj(o_hm, w["w_uv_h"])

            xp3, st_p = _conv_mix(xp.reshape(B, T, D), att_p, u_p.reshape(B, T, conv_ch),
                                  jnp.zeros((B, conv_halo, conv_ch), F32), w, stride=1)
            xp = xp3.reshape(B * T, D)
            hist_s = jnp.pad(state_conv[e], ((0, 0), (conv_halo - (conv_width - 1), 0), (0, 0)))
            xs3, st_s = _conv_mix(_time_major(xs.reshape(DB, TS, D)), _time_major(att_s.reshape(DB, TS, -1)),
                                  _time_major(u_s.reshape(DB, TS, conv_ch)), _time_major(hist_s), w, stride=DB)
            xs = _batch_major(xs3, DB).reshape(DB * TS, D)

            lat_p.append(c_p.reshape(B, T, kv_lora)); kr_p.append(k_rope_p.reshape(B, T, rope))
            lat_s.append(c_s.reshape(DB, TS, kv_lora)); kr_s.append(k_rope_s.reshape(DB, TS, rope))
            conv_p.append(st_p); conv_s.append(_batch_major(st_s, DB))
        else:
            o = l // 2
            g = mix_norm[l][None]
            pw = pool_w[o].astype(BF16)
            ps = pool_scale[o][None]
            xp3, st_p = _pool_mix(xp.reshape(B, T, D), jnp.zeros((B, pool_halo, D), F32), g, pw, ps,
                                  stride=1, windows=windows, pos0=0)
            xp = xp3.reshape(B * T, D)
            hist_s = jnp.pad(state_pool[o], ((0, 0), (pool_halo - pool_state, 0), (0, 0)))
            xs3, st_s = _pool_mix(_time_major(xs.reshape(DB, TS, D)), _time_major(hist_s), g, pw, ps,
                                  stride=DB, windows=windows, pos0=past_len)
            xs = _batch_major(xs3, DB).reshape(DB * TS, D)
            pool_p.append(st_p); pool_s.append(_batch_major(st_s, DB))
        w2 = (ffn2_norm, ffn2_w_gate, ffn2_w_up, ffn2_w_down)
        last = l == depth - 1
        xp = _ffn(xp, *w2, fg, l, final=last)
        xs = _ffn(xs, *w2, fg, l, final=last)

    return (xp.reshape(B, T, D), xs.reshape(DB, TS, D),
            jnp.stack(lat_p), jnp.stack(kr_p), jnp.stack(lat_s), jnp.stack(kr_s),
            jnp.stack(conv_p), jnp.stack(conv_s), jnp.stack(pool_p), jnp.stack(pool_s))
```

```python
import functools

import jax
import jax.numpy as jnp
from jax import lax
from jax.experimental import pallas as pl
from jax.experimental.pallas import tpu as pltpu

EPS = 1e-6
ROPE_BASE = 10000.0
LANES = 128
SUBLANES = 8
HEAD_PAD = 128
NEG_BIG = -1e30
LOG2_E = 1.4426950408889634
VMEM_LIMIT = 56 * 1024 * 1024
BF16 = jnp.bfloat16
F32 = jnp.float32


def _cparams(*sem):
    return pltpu.CompilerParams(dimension_semantics=sem, vmem_limit_bytes=VMEM_LIMIT)


def _resident(shape):
    nd = len(shape)
    return pl.BlockSpec(shape, lambda *_: (0,) * nd, pipeline_mode=pl.Buffered(1))


def _rms(x, g):
    return x * lax.rsqrt(jnp.mean(x * x, axis=-1, keepdims=True) + EPS) * g


def _dot(a, b):
    return jnp.dot(a, b, preferred_element_type=F32)


def _dot_nt(a, b):
    return lax.dot_general(a, b, (((1,), (1,)), ((), ())), preferred_element_type=F32)


def _ffn_body(xp_ref, xs_ref, g_ref, wg_ref, wu_ref, wd_ref, fg_ref, op_ref, os_ref, *, chunk, final, n_prompt):
    def half_step(x_ref, o_ref):
        x = x_ref[...]
        h = _rms(x, g_ref[...]).astype(BF16)
        d_ff = wg_ref.shape[1]
        acc = jnp.zeros(x.shape, F32)
        for c in range(d_ff // chunk):
            sl = slice(c * chunk, (c + 1) * chunk)
            gate = _dot(h, wg_ref[:, sl].astype(BF16))
            up = _dot(h, wu_ref[:, sl].astype(BF16))
            act = (gate * jax.nn.sigmoid(gate) * up).astype(BF16)
            acc = acc + _dot(act, wd_ref[sl, :].astype(BF16))
        y = x + 0.5 * acc
        if final:
            y = _rms(y, fg_ref[...])
        o_ref[...] = y

    i = pl.program_id(0)

    @pl.when(i < n_prompt)
    def _():
        half_step(xp_ref, op_ref)

    @pl.when(i == n_prompt)
    def _():
        half_step(xs_ref, os_ref)


def _ffn(xp, xs, g, wg, wu, wd, fg, layer, *, final=False):
    n, d = xp.shape
    ns = xs.shape[0]
    d_ff = wg.shape[2]
    tm = min(512, n)
    n_prompt = n // tm
    chunk = 256 if d_ff % 256 == 0 else LANES
    prompt_row = lambda i: (jnp.minimum(i, n_prompt - 1), 0)
    whole = lambda i: (0, 0)
    layer_block = lambda shape: pl.BlockSpec((None,) + shape, lambda i: (layer, 0, 0), pipeline_mode=pl.Buffered(1))
    return pl.pallas_call(
        functools.partial(_ffn_body, chunk=chunk, final=final, n_prompt=n_prompt),
        out_shape=[jax.ShapeDtypeStruct((n, d), F32), jax.ShapeDtypeStruct((ns, d), F32)],
        grid=(n_prompt + 1,),
        in_specs=[pl.BlockSpec((tm, d), prompt_row), _resident((ns, d)), layer_block((1, d)),
                  layer_block((d, d_ff)), layer_block((d, d_ff)), layer_block((d_ff, d)), _resident((1, d))],
        out_specs=[pl.BlockSpec((tm, d), prompt_row), pl.BlockSpec((ns, d), whole)],
        compiler_params=_cparams("arbitrary"),
        name="ffn",
    )(xp, xs, g.reshape(g.shape[0], 1, d), wg, wu, wd, fg)


def _rope(v, t1, t2):
    lane = lax.broadcasted_iota(jnp.int32, v.shape, 1)
    partner = jnp.where(lane < 80, pltpu.roll(v, 112, 1), pltpu.roll(v, 16, 1))
    return v * t1 + partner * t2


def _proj_body(x_ref, t1_ref, t2_ref, g_ref, wq_ref, wkv_ref, wkr_ref, wa_ref, wgl_ref, qn_ref, kvn_ref,
               wuq_ref, wk_ref, *rest, absorbed, n_heads):
    if absorbed:
        q_ref, lat_ref, kr_ref, u_ref, qa_ref = rest
    else:
        wv_ref, q_ref, lat_ref, kr_ref, u_ref, k_ref, v_ref = rest
    x = x_ref[...]
    h = _rms(x, g_ref[...]).astype(BF16)
    t1 = t1_ref[...]
    t2 = t2_ref[...]

    u_ref[...] = _dot(h, wa_ref[...]) * jax.nn.sigmoid(_dot(h, wgl_ref[...]))

    c_kv = _rms(_dot(h, wkv_ref[...]), kvn_ref[...])
    lat_ref[...] = c_kv
    c_bf = c_kv.astype(BF16)

    kr = _rope(_dot(h, wkr_ref[...]), t1, t2)
    kr_ref[...] = kr[:, 64:96]

    qn = _rms(_dot(h, wq_ref[...]), qn_ref[...]).astype(BF16)
    q = _dot(qn, wuq_ref[...])
    q_heads = [_rope(q[:, i * HEAD_PAD:(i + 1) * HEAD_PAD], t1, t2) for i in range(n_heads)]

    if absorbed:
        q_heads = [qh.astype(BF16) for qh in q_heads]
        q_ref[...] = jnp.concatenate(q_heads, axis=1)
        kv = wk_ref.shape[2]
        for i in range(n_heads):
            qa_ref[:, i * kv:(i + 1) * kv] = _dot(q_heads[i], wk_ref[i]).astype(BF16)
    else:
        q_ref[0] = jnp.concatenate(q_heads, axis=1).T.astype(BF16)
        k = _dot(c_bf, wk_ref[...])
        k_ref[...] = jnp.concatenate(
            [(k[:, i * HEAD_PAD:(i + 1) * HEAD_PAD] + kr).astype(BF16) for i in range(n_heads)], axis=1)
        v_ref[0] = _dot(c_bf, wv_ref[...]).T.astype(BF16)


def _proj(x, t1, t2, w, *, absorbed, seqs=1):
    n, d = x.shape
    tm = min(512, n // seqs)
    n_heads = w["n_heads"]
    kv = w["w_kv"].shape[1]
    rope = w["rope"]
    conv_ch = w["w_a"].shape[1]
    row = lambda i: (i, 0)
    rows = lambda c, dt: (jax.ShapeDtypeStruct((n, c), dt), pl.BlockSpec((tm, c), row))
    per_seq = (n // seqs) // tm

    def transposed(c):
        return (jax.ShapeDtypeStruct((seqs, c, n // seqs), BF16),
                pl.BlockSpec((1, c, tm), lambda i: (i // per_seq, 0, i % per_seq)))

    weights = [w["mix_norm"], w["w_q"], w["w_kv"], w["w_kr"], w["w_a"], w["w_g"], w["q_norm"], w["kv_norm"],
               w["w_uq"]] + ([w["w_uk_abs"]] if absorbed else [w["w_uk_pad"], w["w_uv"]])
    common = [rows(kv, F32), rows(rope, F32), rows(conv_ch, F32)]
    if absorbed:
        outs = [rows(n_heads * HEAD_PAD, BF16)] + common + [rows(n_heads * kv, BF16)]
    else:
        outs = [transposed(n_heads * HEAD_PAD)] + common + [rows(n_heads * HEAD_PAD, BF16),
                                                            transposed(w["w_uv"].shape[1])]
    return pl.pallas_call(
        functools.partial(_proj_body, absorbed=absorbed, n_heads=n_heads),
        out_shape=[o[0] for o in outs],
        grid=(n // tm,),
        in_specs=[pl.BlockSpec((tm, d), row), pl.BlockSpec((tm, LANES), row), pl.BlockSpec((tm, LANES), row)]
                 + [_resident(a.shape) for a in weights],
        out_specs=[o[1] for o in outs],
        compiler_params=_cparams("parallel"),
        name="proj_abs" if absorbed else "proj",
    )(x, t1, t2, *weights)


def _attn_body(qt_ref, k_ref, vt_ref, o_ref, *, tq, v_dim):
    qi = pl.program_id(2)
    heads = (0, 1)
    qts = [qt_ref[0, hh * HEAD_PAD:(hh + 1) * HEAD_PAD, :] for hh in heads]

    def chunk(j):
        return pl.ds(pl.multiple_of(j * tq, tq), tq)

    def scores(j):
        return tuple(_dot(k_ref[0, chunk(j), hh * HEAD_PAD:(hh + 1) * HEAD_PAD], qts[hh]) for hh in heads)

    def softmax(sts, ms, ls, masked):
        ps, ms_new, ls_new, alphas = [], [], [], []
        for hh in heads:
            st = sts[hh]
            if masked:
                key = lax.broadcasted_iota(jnp.int32, st.shape, 0)
                qry = lax.broadcasted_iota(jnp.int32, st.shape, 1)
                st = jnp.where(key <= qry, st, NEG_BIG)
            m_new = jnp.maximum(ms[hh], jnp.max(st, axis=0, keepdims=True))
            p = jnp.exp2(st - m_new)
            alpha = jnp.exp2(ms[hh] - m_new)
            ps.append(p.astype(BF16))
            ms_new.append(m_new)
            ls_new.append(alpha * ls[hh] + jnp.sum(p, axis=0, keepdims=True))
            alphas.append(alpha)
        return tuple(ps), tuple(ms_new), tuple(ls_new), tuple(alphas)

    def accumulate(j, ps, alphas, accs):
        return tuple(alphas[hh] * accs[hh] + _dot(vt_ref[0, hh * v_dim:(hh + 1) * v_dim, chunk(j)], ps[hh])
                     for hh in heads)

    def step(j, carry, masked):
        ms, ls, accs = carry
        ps, ms, ls, alphas = softmax(scores(j), ms, ls, masked)
        return ms, ls, accumulate(j, ps, alphas, accs)

    both = lambda x: (x, x)
    carry = (both(jnp.full((1, tq), NEG_BIG, F32)), both(jnp.zeros((1, tq), F32)), both(jnp.zeros((v_dim, tq), F32)))
    carry = lax.fori_loop(0, qi, functools.partial(step, masked=False), carry)
    _, ls, accs = step(qi, carry, True)
    o_ref[0] = jnp.concatenate([accs[0] / ls[0], accs[1] / ls[1]], axis=0).T.astype(o_ref.dtype)


def _attn_prompt(qt, k, vt, *, n_heads, v_dim):
    b, t, _ = k.shape
    tq = min(512, t)
    return pl.pallas_call(
        functools.partial(_attn_body, tq=tq, v_dim=v_dim),
        out_shape=jax.ShapeDtypeStruct((b, t, n_heads * v_dim), BF16),
        grid=(b, n_heads // 2, t // tq),
        in_specs=[pl.BlockSpec((1, 2 * HEAD_PAD, tq), lambda bi, hp, qi: (bi, hp, qi)),
                  pl.BlockSpec((1, t, 2 * HEAD_PAD), lambda bi, hp, qi: (bi, 0, hp)),
                  pl.BlockSpec((1, 2 * v_dim, t), lambda bi, hp, qi: (bi, hp, 0))],
        out_specs=pl.BlockSpec((1, tq, 2 * v_dim), lambda bi, hp, qi: (bi, qi, hp)),
        compiler_params=_cparams("parallel", "parallel", "arbitrary"),
        name="attn_prompt",
    )(qt, k, vt)


def _paged_body(pt_ref, qa_ref, qr_ref, cn_ref, kn_ref, lat_hbm, krt_hbm, o_ref, lat_buf, krt_buf, sems,
                *, e, n_pages, n_heads):
    b = pl.program_id(0)
    page = krt_hbm.shape[3]

    def page_copies(seq, slot):
        copies = []
        for pi in range(n_pages):
            pid = pt_ref[seq * n_pages + pi]
            rows = pl.ds(pi * page, page)
            copies.append(pltpu.make_async_copy(lat_hbm.at[e, pid], lat_buf.at[slot, rows], sems.at[0, slot]))
            copies.append(pltpu.make_async_copy(krt_hbm.at[e, pid], krt_buf.at[slot, :, rows], sems.at[1, slot]))
        return copies

    @pl.when(b == 0)
    def _():
        for cp in page_copies(0, 0):
            cp.start()

    @pl.when(b + 1 < pl.num_programs(0))
    def _():
        for cp in page_copies(b + 1, (b + 1) % 2):
            cp.start()

    slot = b % 2
    for cp in page_copies(b, slot):
        cp.wait()

    qa = qa_ref[0]
    qr = qr_ref[0]
    lat = lat_buf[slot].astype(BF16)
    krt = krt_buf[slot].astype(BF16)
    cn = cn_ref[0].astype(BF16)
    kn = kn_ref[0].astype(BF16)
    s = _dot_nt(qa, lat) + _dot(qr, krt)
    sn = _dot_nt(qa, cn) + _dot_nt(qr, kn)
    r = lax.broadcasted_iota(jnp.int32, sn.shape, 0) // n_heads
    c = lax.broadcasted_iota(jnp.int32, sn.shape, 1)
    sn = jnp.where(c <= r, sn, NEG_BIG)
    m = jnp.maximum(jnp.max(s, axis=1, keepdims=True), jnp.max(sn, axis=1, keepdims=True))
    p = jnp.exp2(s - m)
    pn = jnp.exp2(sn - m)
    l = jnp.sum(p, axis=1, keepdims=True) + jnp.sum(pn, axis=1, keepdims=True)
    o = _dot(p.astype(BF16), lat) + _dot(pn.astype(BF16), cn)
    o_ref[0] = (o / l).astype(o_ref.dtype)


def _attn_paged(e, page_table, qa, qr, c_new, k_new, cache_latent, cache_krope_t, *, n_heads):
    db, rows, kv = qa.shape
    rope = qr.shape[2]
    n_pages = page_table.shape[1]
    page = cache_latent.shape[2]
    past = n_pages * page
    new_pad = c_new.shape[1]
    per_b = lambda b, pt: (b, 0, 0)
    grid_spec = pltpu.PrefetchScalarGridSpec(
        num_scalar_prefetch=1,
        grid=(db,),
        in_specs=[pl.BlockSpec((1, rows, kv), per_b), pl.BlockSpec((1, rows, rope), per_b),
                  pl.BlockSpec((1, new_pad, kv), per_b), pl.BlockSpec((1, new_pad, rope), per_b),
                  pl.BlockSpec(memory_space=pl.ANY), pl.BlockSpec(memory_space=pl.ANY)],
        out_specs=pl.BlockSpec((1, rows, kv), per_b),
        scratch_shapes=[pltpu.VMEM((2, past, kv), F32), pltpu.VMEM((2, rope, past), F32),
                        pltpu.SemaphoreType.DMA((2, 2))],
    )
    return pl.pallas_call(
        functools.partial(_paged_body, e=e, n_pages=n_pages, n_heads=n_heads),
        out_shape=jax.ShapeDtypeStruct((db, rows, kv), BF16),
        grid_spec=grid_spec,
        compiler_params=_cparams("arbitrary"),
        name="attn_paged",
    )(page_table.reshape(-1), qa, qr, c_new, k_new, cache_latent, cache_krope_t)


def _uv_body(o_ref, w_ref, a_ref, *, n_heads):
    a_ref[...] = jnp.concatenate([_dot(o_ref[i], w_ref[i]) for i in range(n_heads)], axis=1).astype(a_ref.dtype)


def _uv_proj(o_hm, w_uv_h):
    n_heads, n, kv = o_hm.shape
    v_dim = w_uv_h.shape[2]
    return pl.pallas_call(
        functools.partial(_uv_body, n_heads=n_heads),
        out_shape=jax.ShapeDtypeStruct((n, n_heads * v_dim), BF16),
        grid=(1,),
        in_specs=[_resident(o_hm.shape), _resident(w_uv_h.shape)],
        out_specs=pl.BlockSpec((n, n_heads * v_dim), lambda i: (0, 0)),
        compiler_params=_cparams("arbitrary"),
        name="uv_proj",
    )(o_hm, w_uv_h)


def _conv_body(x_ref, att_ref, u_ref, hist_ref, cw_ref, cb_ref, lg_ref, lb_ref, woa_ref, woc_ref,
               o_ref, st_ref, win_ref, *shift_refs, stride, width, sub):
    ti = pl.program_id(1)
    nt = pl.num_programs(1)
    tt = u_ref.shape[1]
    halo = hist_ref.shape[1]
    pad = halo // stride - (width - 1)

    @pl.when(ti == 0)
    def _():
        win_ref[0:halo, :] = hist_ref[0]

    win_ref[halo:halo + tt, :] = u_ref[0]

    offsets = [(kk + pad) * stride for kk in range(width)]
    shifts = _conv_shifts(offsets)
    for sh, ref in zip(shifts, shift_refs):
        ref[...] = win_ref[sh:sh + ref.shape[0], :]

    def tap(off, r0):
        sh = off % SUBLANES
        src = win_ref if sh == 0 else shift_refs[shifts.index(sh)]
        return src[r0 + off - sh:r0 + off - sh + sub, :]

    cb = cb_ref[...]
    lg = lg_ref[...]
    lb = lb_ref[...]
    cvs = []
    for r0 in range(0, tt, sub):
        y = jnp.zeros((sub, u_ref.shape[2]), F32)
        for kk in range(width):
            y = y + cw_ref[kk:kk + 1, :] * tap(offsets[kk], r0)
        y = y + cb
        yc = y - jnp.mean(y, axis=-1, keepdims=True)
        var = jnp.mean(yc * yc, axis=-1, keepdims=True)
        z = yc * lax.rsqrt(var + EPS) * lg + lb
        cvs.append((z * jax.nn.sigmoid(z)).astype(BF16))
    cv = jnp.concatenate(cvs, axis=0)
    o_ref[0] = x_ref[0] + _dot(att_ref[0], woa_ref[...]) + _dot(cv, woc_ref[...])

    @pl.when(ti == nt - 1)
    def _():
        st_ref[0] = win_ref[tt + pad * stride:tt + halo, :]

    if halo <= tt:
        @pl.when(ti < nt - 1)
        def _():
            win_ref[0:halo, :] = win_ref[tt:tt + halo, :]


def _conv_shifts(offsets):
    return sorted({off % SUBLANES for off in offsets} - {0})


def _conv_mix(x, att, u, hist, w, *, stride):
    bo, t, d = x.shape
    c = u.shape[2]
    a = att.shape[2]
    halo = hist.shape[1]
    width = w["conv_w"].shape[0]
    tt = min(512, t)
    assert t == tt or halo <= tt
    keep = (width - 1) * stride
    pad = halo // stride - (width - 1)
    n_shift = len(_conv_shifts([(kk + pad) * stride for kk in range(width)]))
    tile = lambda b, i: (b, i, 0)
    per_b = lambda b, i: (b, 0, 0)
    weights = [w["conv_w"], w["conv_b"], w["conv_ln_g"], w["conv_ln_b"], w["w_out_att"], w["w_out_conv"]]
    return pl.pallas_call(
        functools.partial(_conv_body, stride=stride, width=width, sub=min(64, tt)),
        out_shape=[jax.ShapeDtypeStruct((bo, t, d), F32), jax.ShapeDtypeStruct((bo, keep, c), F32)],
        grid=(bo, t // tt),
        in_specs=[pl.BlockSpec((1, tt, d), tile), pl.BlockSpec((1, tt, a), tile), pl.BlockSpec((1, tt, c), tile),
                  pl.BlockSpec((1, halo, c), per_b)] + [_resident(v.shape) for v in weights],
        out_specs=[pl.BlockSpec((1, tt, d), tile), pl.BlockSpec((1, keep, c), per_b)],
        scratch_shapes=[pltpu.VMEM((halo + tt, c), F32)] + [pltpu.VMEM((halo + tt - SUBLANES, c), F32)] * n_shift,
        compiler_params=_cparams("parallel", "arbitrary"),
        name="conv_mix",
    )(x, att, u, hist, *weights)


def _pool_body(x_ref, hist_ref, g_ref, pw_ref, ps_ref, o_ref, st_ref, win_ref, *, stride, windows, pos0):
    ti = pl.program_id(1)
    nt = pl.num_programs(1)
    tt = x_ref.shape[1]
    halo = hist_ref.shape[1]
    d = x_ref.shape[2]
    grp = d // len(windows)

    @pl.when(ti == 0)
    def _():
        win_ref[0:halo, :] = hist_ref[0]

    x = x_ref[0]
    h = _rms(x, g_ref[...])
    win_ref[halo:halo + tt, :] = h

    pos = pos0 + ti * (tt // stride) + lax.broadcasted_iota(jnp.int32, (tt, 1), 0) // stride
    ys = []
    for gi, wn in enumerate(windows):
        cols = slice(gi * grp, (gi + 1) * grp)
        s = h[:, cols]
        for jj in range(1, wn):
            s = s + win_ref[halo - jj * stride:halo - jj * stride + tt, cols]
        cnt = jnp.minimum(pos + 1, wn).astype(F32)
        diff = (s / cnt - h[:, cols]).astype(BF16)
        ys.append(_dot(diff, pw_ref[gi]))
    o_ref[0] = x + jnp.concatenate(ys, axis=1) * ps_ref[...]

    keep = st_ref.shape[1]

    @pl.when(ti == nt - 1)
    def _():
        st_ref[0] = win_ref[tt + halo - keep:tt + halo, :]

    if halo <= tt:
        @pl.when(ti < nt - 1)
        def _():
            win_ref[0:halo, :] = win_ref[tt:tt + halo, :]


def _pool_mix(x, hist, g, pool_w, pool_scale, *, stride, windows, pos0):
    bo, t, d = x.shape
    halo = hist.shape[1]
    tt = min(512, t)
    assert t == tt or halo <= tt
    keep = (max(windows) - 1) * stride
    tile = lambda b, i: (b, i, 0)
    per_b = lambda b, i: (b, 0, 0)
    return pl.pallas_call(
        functools.partial(_pool_body, stride=stride, windows=windows, pos0=pos0),
        out_shape=[jax.ShapeDtypeStruct((bo, t, d), F32), jax.ShapeDtypeStruct((bo, keep, d), F32)],
        grid=(bo, t // tt),
        in_specs=[pl.BlockSpec((1, tt, d), tile), pl.BlockSpec((1, halo, d), per_b),
                  _resident(g.shape), _resident(pool_w.shape), _resident(pool_scale.shape)],
        out_specs=[pl.BlockSpec((1, tt, d), tile), pl.BlockSpec((1, keep, d), per_b)],
        scratch_shapes=[pltpu.VMEM((halo + tt, d), F32)],
        compiler_params=_cparams("parallel", "arbitrary"),
        name="pool_mix",
    )(x, hist, g, pool_w, pool_scale)


def _rope_tables(pos, nope, rope):
    inv = ROPE_BASE ** (-jnp.arange(0, rope, 2, dtype=F32) / rope)
    ang = pos[:, None] * inv[None, :]
    cos, sin = jnp.cos(ang), jnp.sin(ang)
    n = pos.shape[0]
    zero_hi = jnp.zeros((n, HEAD_PAD - nope - rope), F32)
    t1 = jnp.concatenate([jnp.ones((n, nope), F32), cos, cos, zero_hi], axis=1)
    t2 = jnp.concatenate([jnp.zeros((n, nope), F32), -sin, sin, zero_hi], axis=1)
    return t1, t2


def _time_major(a):
    b, s, c = a.shape
    return a.transpose(1, 0, 2).reshape(1, s * b, c)


def _batch_major(a, b):
    c = a.shape[2]
    return a.reshape(-1, b, c).transpose(1, 0, 2)


def kernel(x_prompt, x_sample, cache_latent, cache_krope, state_conv, state_pool, page_table, ffn1_norm, ffn1_w_gate, ffn1_w_up, ffn1_w_down, mix_norm, w_in, q_norm, w_uq, kv_norm, w_uk, w_uv, conv_w, conv_b, conv_ln_g, conv_ln_b, w_out, pool_w, pool_scale, ffn2_norm, ffn2_w_gate, ffn2_w_up, ffn2_w_down, final_norm):
    B, T, D = x_prompt.shape
    DB, TS, _ = x_sample.shape
    depth = ffn1_norm.shape[0]
    q_lora, n_heads, qk_dim = w_uq.shape[1:]
    kv_lora, _, nope = w_uk.shape[1:]
    v_dim = w_uv.shape[3]
    rope = qk_dim - nope
    conv_ch = conv_w.shape[2]
    conv_width = conv_w.shape[1]
    n_groups = pool_w.shape[1]
    grp = D // n_groups
    windows = tuple(2 ** (i + 1) for i in range(n_groups))
    pool_state = state_pool.shape[2]
    n_pages = page_table.shape[1]
    page = cache_latent.shape[2]
    past_len = n_pages * page
    scale = float(qk_dim) ** -0.5 * LOG2_E
    assert nope == 64 and rope == 32 and v_dim == 64 and max(windows) - 1 == pool_state and grp * n_groups == D

    t1_p, t2_p = _rope_tables(jnp.arange(T, dtype=F32), nope, rope)
    t1_p, t2_p = jnp.tile(t1_p, (B, 1)), jnp.tile(t2_p, (B, 1))
    t1_s, t2_s = _rope_tables(past_len + jnp.arange(TS, dtype=F32), nope, rope)
    t1_s, t2_s = jnp.tile(t1_s, (DB, 1)), jnp.tile(t2_s, (DB, 1))

    def pad_rope_cols(wr, lead):
        z0 = jnp.zeros(wr.shape[:-1] + (lead,), wr.dtype)
        z1 = jnp.zeros(wr.shape[:-1] + (HEAD_PAD - lead - rope,), wr.dtype)
        return jnp.concatenate([z0, wr, z1], axis=-1)

    def even_weights(l):
        e = l // 2
        wi = w_in[e]
        o = 0
        w_q = wi[:, o:o + q_lora]; o += q_lora
        w_kv = wi[:, o:o + kv_lora]; o += kv_lora
        w_kr = wi[:, o:o + rope]; o += rope
        w_a = wi[:, o:o + conv_ch]; o += conv_ch
        w_g = wi[:, o:o + conv_ch]
        uq = w_uq[e] * scale
        uq = jnp.concatenate([uq, jnp.zeros((q_lora, n_heads, HEAD_PAD - qk_dim), F32)], axis=-1)
        uk = w_uk[e]
        uk_pad = jnp.concatenate([uk, jnp.zeros((kv_lora, n_heads, HEAD_PAD - nope), F32)], axis=-1)
        uk_abs = jnp.concatenate([uk.transpose(1, 2, 0),
                                  jnp.zeros((n_heads, HEAD_PAD - nope, kv_lora), F32)], axis=1)
        return dict(
            n_heads=n_heads, rope=rope,
            mix_norm=mix_norm[l][None], w_q=w_q.astype(BF16), w_kv=w_kv.astype(BF16),
            w_kr=pad_rope_cols(w_kr, nope).astype(BF16), w_a=w_a.astype(BF16), w_g=w_g.astype(BF16),
            q_norm=q_norm[e][None], kv_norm=kv_norm[e][None],
            w_uq=uq.reshape(q_lora, n_heads * HEAD_PAD).astype(BF16),
            w_uk_pad=uk_pad.reshape(kv_lora, n_heads * HEAD_PAD).astype(BF16),
            w_uk_abs=uk_abs.astype(BF16),
            w_uv=w_uv[e].reshape(kv_lora, n_heads * v_dim).astype(BF16),
            w_uv_h=w_uv[e].transpose(1, 0, 2).astype(BF16),
            conv_w=conv_w[e], conv_b=conv_b[e][None], conv_ln_g=conv_ln_g[e][None], conv_ln_b=conv_ln_b[e][None],
            w_out_att=w_out[e][:n_heads * v_dim].astype(BF16), w_out_conv=w_out[e][n_heads * v_dim:].astype(BF16),
        )

    cache_krope_t = jnp.swapaxes(cache_krope, 2, 3)
    xp = x_prompt.reshape(B * T, D)
    xs = x_sample.reshape(DB * TS, D)
    lat_p, kr_p, lat_s, kr_s, conv_p, conv_s, pool_p, pool_s = ([] for _ in range(8))
    conv_halo = 32 if conv_width - 1 <= 32 else conv_width - 1
    pool_halo = pool_state + 1

    for l in range(depth):
        fg = final_norm[None]
        w1 = (ffn1_norm, ffn1_w_gate, ffn1_w_up, ffn1_w_down)
        xp, xs = _ffn(xp, xs, *w1, fg, l)
        if l % 2 == 0:
            e = l // 2
            w = even_weights(l)
            qt_p, c_p, k_rope_p, u_p, k_p, vt_p = _proj(xp, t1_p, t2_p, w, absorbed=False, seqs=B)
            q_s, c_s, k_rope_s, u_s, qa_s = _proj(xs, t1_s, t2_s, w, absorbed=True)

            att_p = _attn_prompt(qt_p, k_p.reshape(B, T, -1), vt_p, n_heads=n_heads, v_dim=v_dim)

            qa = qa_s.reshape(DB, TS * n_heads, kv_lora)
            qr = q_s.reshape(DB, TS * n_heads, HEAD_PAD)[:, :, nope:nope + rope]
            new_pad = 16
            c_new = jnp.pad(c_s.reshape(DB, TS, kv_lora), ((0, 0), (0, new_pad - TS), (0, 0)))
            k_new = jnp.pad(k_rope_s.reshape(DB, TS, rope), ((0, 0), (0, new_pad - TS), (0, 0)))
            o_s = _attn_paged(e, page_table, qa, qr, c_new, k_new, cache_latent, cache_krope_t, n_heads=n_heads)
            o_hm = o_s.reshape(DB * TS, n_heads, kv_lora).transpose(1, 0, 2)
            att_s = _uv_proj(o_hm, w["w_uv_h"])

            xp3, st_p = _conv_mix(xp.reshape(B, T, D), att_p, u_p.reshape(B, T, conv_ch),
                                  jnp.zeros((B, conv_halo, conv_ch), F32), w, stride=1)
            xp = xp3.reshape(B * T, D)
            hist_s = jnp.pad(state_conv[e], ((0, 0), (conv_halo - (conv_width - 1), 0), (0, 0)))
            xs3, st_s = _conv_mix(_time_major(xs.reshape(DB, TS, D)), _time_major(att_s.reshape(DB, TS, -1)),
                                  _time_major(u_s.reshape(DB, TS, conv_ch)), _time_major(hist_s), w, stride=DB)
            xs = _batch_major(xs3, DB).reshape(DB * TS, D)

            lat_p.append(c_p.reshape(B, T, kv_lora)); kr_p.append(k_rope_p.reshape(B, T, rope))
            lat_s.append(c_s.reshape(DB, TS, kv_lora)); kr_s.append(k_rope_s.reshape(DB, TS, rope))
            conv_p.append(st_p); conv_s.append(_batch_major(st_s, DB))
        else:
            o = l // 2
            g = mix_norm[l][None]
            pw = pool_w[o].astype(BF16)
            ps = pool_scale[o][None]
            xp3, st_p = _pool_mix(xp.reshape(B, T, D), jnp.zeros((B, pool_halo, D), F32), g, pw, ps,
                                  stride=1, windows=windows, pos0=0)
            xp = xp3.reshape(B * T, D)
            hist_s = jnp.pad(state_pool[o], ((0, 0), (pool_halo - pool_state, 0), (0, 0)))
            xs3, st_s = _pool_mix(_time_major(xs.reshape(DB, TS, D)), _time_major(hist_s), g, pw, ps,
                                  stride=DB, windows=windows, pos0=past_len)
            xs = _batch_major(xs3, DB).reshape(DB * TS, D)
            pool_p.append(st_p); pool_s.append(_batch_major(st_s, DB))
        w2 = (ffn2_norm, ffn2_w_gate, ffn2_w_up, ffn2_w_down)
        last = l == depth - 1
        xp, xs = _ffn(xp, xs, *w2, fg, l, final=last)

    return (xp.reshape(B, T, D), xs.reshape(DB, TS, D),
            jnp.stack(lat_p), jnp.stack(kr_p), jnp.stack(lat_s), jnp.stack(kr_s),
            jnp.stack(conv_p), jnp.stack(conv_s), jnp.stack(pool_p), jnp.stack(pool_s))
```

```python
import functools

import jax
import jax.numpy as jnp
from jax import lax
from jax.experimental import pallas as pl
from jax.experimental.pallas import tpu as pltpu

EPS = 1e-6
ROPE_BASE = 10000.0
LANES = 128
SUBLANES = 8
HEAD_PAD = 128
NEG_BIG = -1e30
LOG2_E = 1.4426950408889634
VMEM_LIMIT = 56 * 1024 * 1024
BF16 = jnp.bfloat16
F32 = jnp.float32


def _cparams(*sem):
    return pltpu.CompilerParams(dimension_semantics=sem, vmem_limit_bytes=VMEM_LIMIT)


def _resident(shape):
    nd = len(shape)
    return pl.BlockSpec(shape, lambda *_: (0,) * nd, pipeline_mode=pl.Buffered(1))


def _rms(x, g):
    return x * lax.rsqrt(jnp.mean(x * x, axis=-1, keepdims=True) + EPS) * g


def _dot(a, b):
    return jnp.dot(a, b, preferred_element_type=F32)


def _dot_nt(a, b):
    return lax.dot_general(a, b, (((1,), (1,)), ((), ())), preferred_element_type=F32)


def _ffn_body(xp_ref, xs_ref, g_ref, wg_ref, wu_ref, wd_ref, fg_ref, op_ref, os_ref, *, chunk, final, n_prompt):
    def half_step(x_ref, o_ref):
        x = x_ref[...]
        h = _rms(x, g_ref[...]).astype(BF16)
        d_ff = wg_ref.shape[1]
        acc = jnp.zeros(x.shape, F32)
        for c in range(d_ff // chunk):
            sl = slice(c * chunk, (c + 1) * chunk)
            gate = _dot(h, wg_ref[:, sl].astype(BF16))
            up = _dot(h, wu_ref[:, sl].astype(BF16))
            act = (gate * jax.nn.sigmoid(gate) * up).astype(BF16)
            acc = acc + _dot(act, wd_ref[sl, :].astype(BF16))
        y = x + 0.5 * acc
        if final:
            y = _rms(y, fg_ref[...])
        o_ref[...] = y

    i = pl.program_id(0)

    @pl.when(i < n_prompt)
    def _():
        half_step(xp_ref, op_ref)

    @pl.when(i == n_prompt)
    def _():
        half_step(xs_ref, os_ref)


def _ffn(xp, xs, g, wg, wu, wd, fg, layer, *, final=False):
    n, d = xp.shape
    ns = xs.shape[0]
    d_ff = wg.shape[2]
    tm = min(512, n)
    n_prompt = n // tm
    chunk = 256 if d_ff % 256 == 0 else LANES
    prompt_row = lambda i: (jnp.minimum(i, n_prompt - 1), 0)
    whole = lambda i: (0, 0)
    layer_block = lambda shape: pl.BlockSpec((None,) + shape, lambda i: (layer, 0, 0), pipeline_mode=pl.Buffered(1))
    return pl.pallas_call(
        functools.partial(_ffn_body, chunk=chunk, final=final, n_prompt=n_prompt),
        out_shape=[jax.ShapeDtypeStruct((n, d), F32), jax.ShapeDtypeStruct((ns, d), F32)],
        grid=(n_prompt + 1,),
        in_specs=[pl.BlockSpec((tm, d), prompt_row), _resident((ns, d)), layer_block((1, d)),
                  layer_block((d, d_ff)), layer_block((d, d_ff)), layer_block((d_ff, d)), _resident((1, d))],
        out_specs=[pl.BlockSpec((tm, d), prompt_row), pl.BlockSpec((ns, d), whole)],
        compiler_params=_cparams("arbitrary"),
        name="ffn",
    )(xp, xs, g.reshape(g.shape[0], 1, d), wg, wu, wd, fg)


def _rope(v, t1, t2):
    lane = lax.broadcasted_iota(jnp.int32, v.shape, 1)
    partner = jnp.where(lane < 80, pltpu.roll(v, 112, 1), pltpu.roll(v, 16, 1))
    return v * t1 + partner * t2


def _proj_body(x_ref, t1_ref, t2_ref, g_ref, wq_ref, wkv_ref, wkr_ref, wa_ref, wgl_ref, qn_ref, kvn_ref,
               wuq_ref, wk_ref, *rest, absorbed, n_heads):
    if absorbed:
        q_ref, lat_ref, kr_ref, u_ref, qa_ref = rest
    else:
        wv_ref, q_ref, lat_ref, kr_ref, u_ref, k_ref, v_ref = rest
    x = x_ref[...]
    h = _rms(x, g_ref[...]).astype(BF16)
    t1 = t1_ref[...]
    t2 = t2_ref[...]

    u_ref[...] = _dot(h, wa_ref[...]) * jax.nn.sigmoid(_dot(h, wgl_ref[...]))

    c_kv = _rms(_dot(h, wkv_ref[...]), kvn_ref[...])
    lat_ref[...] = c_kv
    c_bf = c_kv.astype(BF16)

    kr = _rope(_dot(h, wkr_ref[...]), t1, t2)
    kr_ref[...] = kr[:, 64:96]

    qn = _rms(_dot(h, wq_ref[...]), qn_ref[...]).astype(BF16)
    q = _dot(qn, wuq_ref[...])
    q_heads = [_rope(q[:, i * HEAD_PAD:(i + 1) * HEAD_PAD], t1, t2) for i in range(n_heads)]

    if absorbed:
        q_heads = [qh.astype(BF16) for qh in q_heads]
        q_ref[...] = jnp.concatenate(q_heads, axis=1)
        kv = wk_ref.shape[2]
        for i in range(n_heads):
            qa_ref[:, i * kv:(i + 1) * kv] = _dot(q_heads[i], wk_ref[i]).astype(BF16)
    else:
        q_ref[0] = jnp.concatenate(q_heads, axis=1).T.astype(BF16)
        k = _dot(c_bf, wk_ref[...])
        k_ref[...] = jnp.concatenate(
            [(k[:, i * HEAD_PAD:(i + 1) * HEAD_PAD] + kr).astype(BF16) for i in range(n_heads)], axis=1)
        v_ref[0] = _dot(c_bf, wv_ref[...]).T.astype(BF16)


def _proj(x, t1, t2, w, *, absorbed, seqs=1):
    n, d = x.shape
    tm = min(512, n // seqs)
    n_heads = w["n_heads"]
    kv = w["w_kv"].shape[1]
    rope = w["rope"]
    conv_ch = w["w_a"].shape[1]
    row = lambda i: (i, 0)
    rows = lambda c, dt: (jax.ShapeDtypeStruct((n, c), dt), pl.BlockSpec((tm, c), row))
    per_seq = (n // seqs) // tm

    def transposed(c):
        return (jax.ShapeDtypeStruct((seqs, c, n // seqs), BF16),
                pl.BlockSpec((1, c, tm), lambda i: (i // per_seq, 0, i % per_seq)))

    weights = [w["mix_norm"], w["w_q"], w["w_kv"], w["w_kr"], w["w_a"], w["w_g"], w["q_norm"], w["kv_norm"],
               w["w_uq"]] + ([w["w_uk_abs"]] if absorbed else [w["w_uk_pad"], w["w_uv"]])
    common = [rows(kv, F32), rows(rope, F32), rows(conv_ch, F32)]
    if absorbed:
        outs = [rows(n_heads * HEAD_PAD, BF16)] + common + [rows(n_heads * kv, BF16)]
    else:
        outs = [transposed(n_heads * HEAD_PAD)] + common + [rows(n_heads * HEAD_PAD, BF16),
                                                            transposed(w["w_uv"].shape[1])]
    return pl.pallas_call(
        functools.partial(_proj_body, absorbed=absorbed, n_heads=n_heads),
        out_shape=[o[0] for o in outs],
        grid=(n // tm,),
        in_specs=[pl.BlockSpec((tm, d), row), pl.BlockSpec((tm, LANES), row), pl.BlockSpec((tm, LANES), row)]
                 + [_resident(a.shape) for a in weights],
        out_specs=[o[1] for o in outs],
        compiler_params=_cparams("parallel"),
        name="proj_abs" if absorbed else "proj",
    )(x, t1, t2, *weights)


def _attn_body(qt_ref, k_ref, vt_ref, o_ref, *, tq, v_dim):
    qi = pl.program_id(2)
    heads = (0, 1)
    qts = [qt_ref[0, hh * HEAD_PAD:(hh + 1) * HEAD_PAD, :] for hh in heads]

    def chunk(j):
        return pl.ds(pl.multiple_of(j * tq, tq), tq)

    def scores(j):
        return tuple(_dot(k_ref[0, chunk(j), hh * HEAD_PAD:(hh + 1) * HEAD_PAD], qts[hh]) for hh in heads)

    def softmax(sts, ms, ls, masked):
        ps, ms_new, ls_new, alphas = [], [], [], []
        for hh in heads:
            st = sts[hh]
            if masked:
                key = lax.broadcasted_iota(jnp.int32, st.shape, 0)
                qry = lax.broadcasted_iota(jnp.int32, st.shape, 1)
                st = jnp.where(key <= qry, st, NEG_BIG)
            m_new = jnp.maximum(ms[hh], jnp.max(st, axis=0, keepdims=True))
            p = jnp.exp2(st - m_new)
            alpha = jnp.exp2(ms[hh] - m_new)
            ps.append(p.astype(BF16))
            ms_new.append(m_new)
            ls_new.append(alpha * ls[hh] + jnp.sum(p, axis=0, keepdims=True))
            alphas.append(alpha)
        return tuple(ps), tuple(ms_new), tuple(ls_new), tuple(alphas)

    def accumulate(j, ps, alphas, accs):
        return tuple(alphas[hh] * accs[hh] + _dot(vt_ref[0, hh * v_dim:(hh + 1) * v_dim, chunk(j)], ps[hh])
                     for hh in heads)

    def step(j, carry, masked):
        ms, ls, accs = carry
        ps, ms, ls, alphas = softmax(scores(j), ms, ls, masked)
        return ms, ls, accumulate(j, ps, alphas, accs)

    def pair(j, carry, second_masked):
        ms, ls, accs = carry
        sts_a, sts_b = scores(j), scores(j + 1)
        ps, ms, ls, alphas = softmax(sts_a, ms, ls, False)
        accs = accumulate(j, ps, alphas, accs)
        ps, ms, ls, alphas = softmax(sts_b, ms, ls, second_masked)
        return ms, ls, accumulate(j + 1, ps, alphas, accs)

    both = lambda x: (x, x)
    carry = (both(jnp.full((1, tq), NEG_BIG, F32)), both(jnp.zeros((1, tq), F32)), both(jnp.zeros((v_dim, tq), F32)))
    carry = lax.fori_loop(0, qi // 2, lambda jj, c: pair(2 * jj, c, False), carry)
    _, ls, accs = lax.cond(qi % 2 == 1, lambda c: pair(qi - 1, c, True), lambda c: step(qi, c, True), carry)
    o_ref[0] = jnp.concatenate([accs[0] / ls[0], accs[1] / ls[1]], axis=0).T.astype(o_ref.dtype)


def _attn_prompt(qt, k, vt, *, n_heads, v_dim):
    b, t, _ = k.shape
    tq = min(512, t)
    return pl.pallas_call(
        functools.partial(_attn_body, tq=tq, v_dim=v_dim),
        out_shape=jax.ShapeDtypeStruct((b, t, n_heads * v_dim), BF16),
        grid=(b, n_heads // 2, t // tq),
        in_specs=[pl.BlockSpec((1, 2 * HEAD_PAD, tq), lambda bi, hp, qi: (bi, hp, qi)),
                  pl.BlockSpec((1, t, 2 * HEAD_PAD), lambda bi, hp, qi: (bi, 0, hp)),
                  pl.BlockSpec((1, 2 * v_dim, t), lambda bi, hp, qi: (bi, hp, 0))],
        out_specs=pl.BlockSpec((1, tq, 2 * v_dim), lambda bi, hp, qi: (bi, qi, hp)),
        compiler_params=_cparams("parallel", "parallel", "arbitrary"),
        name="attn_prompt",
    )(qt, k, vt)


def _paged_body(pt_ref, qa_ref, qr_ref, cn_ref, kn_ref, lat_hbm, krt_hbm, o_ref, lat_buf, krt_buf, sems,
                *, e, n_pages, n_heads):
    b = pl.program_id(0)
    page = krt_hbm.shape[3]

    def page_copies(seq, slot):
        copies = []
        for pi in range(n_pages):
            pid = pt_ref[seq * n_pages + pi]
            rows = pl.ds(pi * page, page)
            copies.append(pltpu.make_async_copy(lat_hbm.at[e, pid], lat_buf.at[slot, rows], sems.at[0, slot]))
            copies.append(pltpu.make_async_copy(krt_hbm.at[e, pid], krt_buf.at[slot, :, rows], sems.at[1, slot]))
        return copies

    @pl.when(b == 0)
    def _():
        for cp in page_copies(0, 0):
            cp.start()

    @pl.when(b + 1 < pl.num_programs(0))
    def _():
        for cp in page_copies(b + 1, (b + 1) % 2):
            cp.start()

    slot = b % 2
    for cp in page_copies(b, slot):
        cp.wait()

    qa = qa_ref[0]
    qr = qr_ref[0]
    lat = lat_buf[slot].astype(BF16)
    krt = krt_buf[slot].astype(BF16)
    cn = cn_ref[0].astype(BF16)
    kn = kn_ref[0].astype(BF16)
    s = _dot_nt(qa, lat) + _dot(qr, krt)
    sn = _dot_nt(qa, cn) + _dot_nt(qr, kn)
    r = lax.broadcasted_iota(jnp.int32, sn.shape, 0) // n_heads
    c = lax.broadcasted_iota(jnp.int32, sn.shape, 1)
    sn = jnp.where(c <= r, sn, NEG_BIG)
    m = jnp.maximum(jnp.max(s, axis=1, keepdims=True), jnp.max(sn, axis=1, keepdims=True))
    p = jnp.exp2(s - m)
    pn = jnp.exp2(sn - m)
    l = jnp.sum(p, axis=1, keepdims=True) + jnp.sum(pn, axis=1, keepdims=True)
    o = _dot(p.astype(BF16), lat) + _dot(pn.astype(BF16), cn)
    o_ref[0] = (o / l).astype(o_ref.dtype)


def _attn_paged(e, page_table, qa, qr, c_new, k_new, cache_latent, cache_krope_t, *, n_heads):
    db, rows, kv = qa.shape
    rope = qr.shape[2]
    n_pages = page_table.shape[1]
    page = cache_latent.shape[2]
    past = n_pages * page
    new_pad = c_new.shape[1]
    per_b = lambda b, pt: (b, 0, 0)
    grid_spec = pltpu.PrefetchScalarGridSpec(
        num_scalar_prefetch=1,
        grid=(db,),
        in_specs=[pl.BlockSpec((1, rows, kv), per_b), pl.BlockSpec((1, rows, rope), per_b),
                  pl.BlockSpec((1, new_pad, kv), per_b), pl.BlockSpec((1, new_pad, rope), per_b),
                  pl.BlockSpec(memory_space=pl.ANY), pl.BlockSpec(memory_space=pl.ANY)],
        out_specs=pl.BlockSpec((1, rows, kv), per_b),
        scratch_shapes=[pltpu.VMEM((2, past, kv), F32), pltpu.VMEM((2, rope, past), F32),
                        pltpu.SemaphoreType.DMA((2, 2))],
    )
    return pl.pallas_call(
        functools.partial(_paged_body, e=e, n_pages=n_pages, n_heads=n_heads),
        out_shape=jax.ShapeDtypeStruct((db, rows, kv), BF16),
        grid_spec=grid_spec,
        compiler_params=_cparams("arbitrary"),
        name="attn_paged",
    )(page_table.reshape(-1), qa, qr, c_new, k_new, cache_latent, cache_krope_t)


def _uv_body(o_ref, w_ref, a_ref, *, n_heads):
    a_ref[...] = jnp.concatenate([_dot(o_ref[i], w_ref[i]) for i in range(n_heads)], axis=1).astype(a_ref.dtype)


def _uv_proj(o_hm, w_uv_h):
    n_heads, n, kv = o_hm.shape
    v_dim = w_uv_h.shape[2]
    return pl.pallas_call(
        functools.partial(_uv_body, n_heads=n_heads),
        out_shape=jax.ShapeDtypeStruct((n, n_heads * v_dim), BF16),
        grid=(1,),
        in_specs=[_resident(o_hm.shape), _resident(w_uv_h.shape)],
        out_specs=pl.BlockSpec((n, n_heads * v_dim), lambda i: (0, 0)),
        compiler_params=_cparams("arbitrary"),
        name="uv_proj",
    )(o_hm, w_uv_h)


def _conv_body(x_ref, att_ref, u_ref, hist_ref, cw_ref, cb_ref, lg_ref, lb_ref, woa_ref, woc_ref,
               o_ref, st_ref, win_ref, *shift_refs, stride, width, sub):
    ti = pl.program_id(1)
    nt = pl.num_programs(1)
    tt = u_ref.shape[1]
    halo = hist_ref.shape[1]
    pad = halo // stride - (width - 1)

    @pl.when(ti == 0)
    def _():
        win_ref[0:halo, :] = hist_ref[0]

    win_ref[halo:halo + tt, :] = u_ref[0]

    offsets = [(kk + pad) * stride for kk in range(width)]
    shifts = _conv_shifts(offsets)
    for sh, ref in zip(shifts, shift_refs):
        ref[...] = win_ref[sh:sh + ref.shape[0], :]

    def tap(off, r0):
        sh = off % SUBLANES
        src = win_ref if sh == 0 else shift_refs[shifts.index(sh)]
        return src[r0 + off - sh:r0 + off - sh + sub, :]

    cb = cb_ref[...]
    lg = lg_ref[...]
    lb = lb_ref[...]
    cvs = []
    for r0 in range(0, tt, sub):
        y = jnp.zeros((sub, u_ref.shape[2]), F32)
        for kk in range(width):
            y = y + cw_ref[kk:kk + 1, :] * tap(offsets[kk], r0)
        y = y + cb
        yc = y - jnp.mean(y, axis=-1, keepdims=True)
        var = jnp.mean(yc * yc, axis=-1, keepdims=True)
        z = yc * lax.rsqrt(var + EPS) * lg + lb
        cvs.append((z * jax.nn.sigmoid(z)).astype(BF16))
    cv = jnp.concatenate(cvs, axis=0)
    o_ref[0] = x_ref[0] + _dot(att_ref[0], woa_ref[...]) + _dot(cv, woc_ref[...])

    @pl.when(ti == nt - 1)
    def _():
        st_ref[0] = win_ref[tt + pad * stride:tt + halo, :]

    if halo <= tt:
        @pl.when(ti < nt - 1)
        def _():
            win_ref[0:halo, :] = win_ref[tt:tt + halo, :]


def _conv_shifts(offsets):
    return sorted({off % SUBLANES for off in offsets} - {0})


def _conv_mix(x, att, u, hist, w, *, stride):
    bo, t, d = x.shape
    c = u.shape[2]
    a = att.shape[2]
    halo = hist.shape[1]
    width = w["conv_w"].shape[0]
    tt = min(512, t)
    assert t == tt or halo <= tt
    keep = (width - 1) * stride
    pad = halo // stride - (width - 1)
    n_shift = len(_conv_shifts([(kk + pad) * stride for kk in range(width)]))
    tile = lambda b, i: (b, i, 0)
    per_b = lambda b, i: (b, 0, 0)
    weights = [w["conv_w"], w["conv_b"], w["conv_ln_g"], w["conv_ln_b"], w["w_out_att"], w["w_out_conv"]]
    return pl.pallas_call(
        functools.partial(_conv_body, stride=stride, width=width, sub=min(64, tt)),
        out_shape=[jax.ShapeDtypeStruct((bo, t, d), F32), jax.ShapeDtypeStruct((bo, keep, c), F32)],
        grid=(bo, t // tt),
        in_specs=[pl.BlockSpec((1, tt, d), tile), pl.BlockSpec((1, tt, a), tile), pl.BlockSpec((1, tt, c), tile),
                  pl.BlockSpec((1, halo, c), per_b)] + [_resident(v.shape) for v in weights],
        out_specs=[pl.BlockSpec((1, tt, d), tile), pl.BlockSpec((1, keep, c), per_b)],
        scratch_shapes=[pltpu.VMEM((halo + tt, c), F32)] + [pltpu.VMEM((halo + tt - SUBLANES, c), F32)] * n_shift,
        compiler_params=_cparams("parallel", "arbitrary"),
        name="conv_mix",
    )(x, att, u, hist, *weights)


def _pool_body(x_ref, hist_ref, g_ref, pw_ref, ps_ref, o_ref, st_ref, win_ref, *, stride, windows, pos0):
    ti = pl.program_id(1)
    nt = pl.num_programs(1)
    tt = x_ref.shape[1]
    halo = hist_ref.shape[1]
    d = x_ref.shape[2]
    grp = d // len(windows)

    @pl.when(ti == 0)
    def _():
        win_ref[0:halo, :] = hist_ref[0]

    x = x_ref[0]
    h = _rms(x, g_ref[...])
    win_ref[halo:halo + tt, :] = h

    pos = pos0 + ti * (tt // stride) + lax.broadcasted_iota(jnp.int32, (tt, 1), 0) // stride
    ys = []
    for gi, wn in enumerate(windows):
        cols = slice(gi * grp, (gi + 1) * grp)
        s = h[:, cols]
        for jj in range(1, wn):
            s = s + win_ref[halo - jj * stride:halo - jj * stride + tt, cols]
        cnt = jnp.minimum(pos + 1, wn).astype(F32)
        diff = (s / cnt - h[:, cols]).astype(BF16)
        ys.append(_dot(diff, pw_ref[gi]))
    o_ref[0] = x + jnp.concatenate(ys, axis=1) * ps_ref[...]

    keep = st_ref.shape[1]

    @pl.when(ti == nt - 1)
    def _():
        st_ref[0] = win_ref[tt + halo - keep:tt + halo, :]

    if halo <= tt:
        @pl.when(ti < nt - 1)
        def _():
            win_ref[0:halo, :] = win_ref[tt:tt + halo, :]


def _pool_mix(x, hist, g, pool_w, pool_scale, *, stride, windows, pos0):
    bo, t, d = x.shape
    halo = hist.shape[1]
    tt = min(512, t)
    assert t == tt or halo <= tt
    keep = (max(windows) - 1) * stride
    tile = lambda b, i: (b, i, 0)
    per_b = lambda b, i: (b, 0, 0)
    return pl.pallas_call(
        functools.partial(_pool_body, stride=stride, windows=windows, pos0=pos0),
        out_shape=[jax.ShapeDtypeStruct((bo, t, d), F32), jax.ShapeDtypeStruct((bo, keep, d), F32)],
        grid=(bo, t // tt),
        in_specs=[pl.BlockSpec((1, tt, d), tile), pl.BlockSpec((1, halo, d), per_b),
                  _resident(g.shape), _resident(pool_w.shape), _resident(pool_scale.shape)],
        out_specs=[pl.BlockSpec((1, tt, d), tile), pl.BlockSpec((1, keep, d), per_b)],
        scratch_shapes=[pltpu.VMEM((halo + tt, d), F32)],
        compiler_params=_cparams("parallel", "arbitrary"),
        name="pool_mix",
    )(x, hist, g, pool_w, pool_scale)


def _rope_tables(pos, nope, rope):
    inv = ROPE_BASE ** (-jnp.arange(0, rope, 2, dtype=F32) / rope)
    ang = pos[:, None] * inv[None, :]
    cos, sin = jnp.cos(ang), jnp.sin(ang)
    n = pos.shape[0]
    zero_hi = jnp.zeros((n, HEAD_PAD - nope - rope), F32)
    t1 = jnp.concatenate([jnp.ones((n, nope), F32), cos, cos, zero_hi], axis=1)
    t2 = jnp.concatenate([jnp.zeros((n, nope), F32), -sin, sin, zero_hi], axis=1)
    return t1, t2


def _time_major(a):
    b, s, c = a.shape
    return a.transpose(1, 0, 2).reshape(1, s * b, c)


def _batch_major(a, b):
    c = a.shape[2]
    return a.reshape(-1, b, c).transpose(1, 0, 2)


def kernel(x_prompt, x_sample, cache_latent, cache_krope, state_conv, state_pool, page_table, ffn1_norm, ffn1_w_gate, ffn1_w_up, ffn1_w_down, mix_norm, w_in, q_norm, w_uq, kv_norm, w_uk, w_uv, conv_w, conv_b, conv_ln_g, conv_ln_b, w_out, pool_w, pool_scale, ffn2_norm, ffn2_w_gate, ffn2_w_up, ffn2_w_down, final_norm):
    B, T, D = x_prompt.shape
    DB, TS, _ = x_sample.shape
    depth = ffn1_norm.shape[0]
    q_lora, n_heads, qk_dim = w_uq.shape[1:]
    kv_lora, _, nope = w_uk.shape[1:]
    v_dim = w_uv.shape[3]
    rope = qk_dim - nope
    conv_ch = conv_w.shape[2]
    conv_width = conv_w.shape[1]
    n_groups = pool_w.shape[1]
    grp = D // n_groups
    windows = tuple(2 ** (i + 1) for i in range(n_groups))
    pool_state = state_pool.shape[2]
    n_pages = page_table.shape[1]
    page = cache_latent.shape[2]
    past_len = n_pages * page
    scale = float(qk_dim) ** -0.5 * LOG2_E
    assert nope == 64 and rope == 32 and v_dim == 64 and max(windows) - 1 == pool_state and grp * n_groups == D

    t1_p, t2_p = _rope_tables(jnp.arange(T, dtype=F32), nope, rope)
    t1_p, t2_p = jnp.tile(t1_p, (B, 1)), jnp.tile(t2_p, (B, 1))
    t1_s, t2_s = _rope_tables(past_len + jnp.arange(TS, dtype=F32), nope, rope)
    t1_s, t2_s = jnp.tile(t1_s, (DB, 1)), jnp.tile(t2_s, (DB, 1))

    def pad_rope_cols(wr, lead):
        z0 = jnp.zeros(wr.shape[:-1] + (lead,), wr.dtype)
        z1 = jnp.zeros(wr.shape[:-1] + (HEAD_PAD - lead - rope,), wr.dtype)
        return jnp.concatenate([z0, wr, z1], axis=-1)

    def even_weights(l):
        e = l // 2
        wi = w_in[e]
        o = 0
        w_q = wi[:, o:o + q_lora]; o += q_lora
        w_kv = wi[:, o:o + kv_lora]; o += kv_lora
        w_kr = wi[:, o:o + rope]; o += rope
        w_a = wi[:, o:o + conv_ch]; o += conv_ch
        w_g = wi[:, o:o + conv_ch]
        uq = w_uq[e] * scale
        uq = jnp.concatenate([uq, jnp.zeros((q_lora, n_heads, HEAD_PAD - qk_dim), F32)], axis=-1)
        uk = w_uk[e]
        uk_pad = jnp.concatenate([uk, jnp.zeros((kv_lora, n_heads, HEAD_PAD - nope), F32)], axis=-1)
        uk_abs = jnp.concatenate([uk.transpose(1, 2, 0),
                                  jnp.zeros((n_heads, HEAD_PAD - nope, kv_lora), F32)], axis=1)
        return dict(
            n_heads=n_heads, rope=rope,
            mix_norm=mix_norm[l][None], w_q=w_q.astype(BF16), w_kv=w_kv.astype(BF16),
            w_kr=pad_rope_cols(w_kr, nope).astype(BF16), w_a=w_a.astype(BF16), w_g=w_g.astype(BF16),
            q_norm=q_norm[e][None], kv_norm=kv_norm[e][None],
            w_uq=uq.reshape(q_lora, n_heads * HEAD_PAD).astype(BF16),
            w_uk_pad=uk_pad.reshape(kv_lora, n_heads * HEAD_PAD).astype(BF16),
            w_uk_abs=uk_abs.astype(BF16),
            w_uv=w_uv[e].reshape(kv_lora, n_heads * v_dim).astype(BF16),
            w_uv_h=w_uv[e].transpose(1, 0, 2).astype(BF16),
            conv_w=conv_w[e], conv_b=conv_b[e][None], conv_ln_g=conv_ln_g[e][None], conv_ln_b=conv_ln_b[e][None],
            w_out_att=w_out[e][:n_heads * v_dim].astype(BF16), w_out_conv=w_out[e][n_heads * v_dim:].astype(BF16),
        )

    cache_krope_t = jnp.swapaxes(cache_krope, 2, 3)
    xp = x_prompt.reshape(B * T, D)
    xs = x_sample.reshape(DB * TS, D)
    lat_p, kr_p, lat_s, kr_s, conv_p, conv_s, pool_p, pool_s = ([] for _ in range(8))
    conv_halo = 32 if conv_width - 1 <= 32 else conv_width - 1
    pool_halo = pool_state + 1

    for l in range(depth):
        fg = final_norm[None]
        w1 = (ffn1_norm, ffn1_w_gate, ffn1_w_up, ffn1_w_down)
        xp, xs = _ffn(xp, xs, *w1, fg, l)
        if l % 2 == 0:
            e = l // 2
            w = even_weights(l)
            qt_p, c_p, k_rope_p, u_p, k_p, vt_p = _proj(xp, t1_p, t2_p, w, absorbed=False, seqs=B)
            q_s, c_s, k_rope_s, u_s, qa_s = _proj(xs, t1_s, t2_s, w, absorbed=True)

            att_p = _attn_prompt(qt_p, k_p.reshape(B, T, -1), vt_p, n_heads=n_heads, v_dim=v_dim)

            qa = qa_s.reshape(DB, TS * n_heads, kv_lora)
            qr = q_s.reshape(DB, TS * n_heads, HEAD_PAD)[:, :, nope:nope + rope]
            new_pad = 16
            c_new = jnp.pad(c_s.reshape(DB, TS, kv_lora), ((0, 0), (0, new_pad - TS), (0, 0)))
            k_new = jnp.pad(k_rope_s.reshape(DB, TS, rope), ((0, 0), (0, new_pad - TS), (0, 0)))
            o_s = _attn_paged(e, page_table, qa, qr, c_new, k_new, cache_latent, cache_krope_t, n_heads=n_heads)
            o_hm = o_s.reshape(DB * TS, n_heads, kv_lora).transpose(1, 0, 2)
            att_s = _uv_proj(o_hm, w["w_uv_h"])

            xp3, st_p = _conv_mix(xp.reshape(B, T, D), att_p, u_p.reshape(B, T, conv_ch),
                                  jnp.zeros((B, conv_halo, conv_ch), F32), w, stride=1)
            xp = xp3.reshape(B * T, D)
            hist_s = jnp.pad(state_conv[e], ((0, 0), (conv_halo - (conv_width - 1), 0), (0, 0)))
            xs3, st_s = _conv_mix(_time_major(xs.reshape(DB, TS, D)), _time_major(att_s.reshape(DB, TS, -1)),
                                  _time_major(u_s.reshape(DB, TS, conv_ch)), _time_major(hist_s), w, stride=DB)
            xs = _batch_major(xs3, DB).reshape(DB * TS, D)

            lat_p.append(c_p.reshape(B, T, kv_lora)); kr_p.append(k_rope_p.reshape(B, T, rope))
            lat_s.append(c_s.reshape(DB, TS, kv_lora)); kr_s.append(k_rope_s.reshape(DB, TS, rope))
            conv_p.append(st_p); conv_s.append(_batch_major(st_s, DB))
        else:
            o = l // 2
            g = mix_norm[l][None]
            pw = pool_w[o].astype(BF16)
            ps = pool_scale[o][None]
            xp3, st_p = _pool_mix(xp.reshape(B, T, D), jnp.zeros((B, pool_halo, D), F32), g, pw, ps,
                                  stride=1, windows=windows, pos0=0)
            xp = xp3.reshape(B * T, D)
            hist_s = jnp.pad(state_pool[o], ((0, 0), (pool_halo - pool_state, 0), (0, 0)))
            xs3, st_s = _pool_mix(_time_major(xs.reshape(DB, TS, D)), _time_major(hist_s), g, pw, ps,
                                  stride=DB, windows=windows, pos0=past_len)
            xs = _batch_major(xs3, DB).reshape(DB * TS, D)
            pool_p.append(st_p); pool_s.append(_batch_major(st_s, DB))
        w2 = (ffn2_norm, ffn2_w_gate, ffn2_w_up, ffn2_w_down)
        last = l == depth - 1
        xp, xs = _ffn(xp, xs, *w2, fg, l, final=last)

    return (xp.reshape(B, T, D), xs.reshape(DB, TS, D),
            jnp.stack(lat_p), jnp.stack(kr_p), jnp.stack(lat_s), jnp.stack(kr_s),
            jnp.stack(conv_p), jnp.stack(conv_s), jnp.stack(pool_p), jnp.stack(pool_s))
```

```python
import functools

import jax
import jax.numpy as jnp
from jax import lax
from jax.experimental import pallas as pl
from jax.experimental.pallas import tpu as pltpu

EPS = 1e-6
ROPE_BASE = 10000.0
LANES = 128
SUBLANES = 8
HEAD_PAD = 128
NEG_BIG = -1e30
LOG2_E = 1.4426950408889634
VMEM_LIMIT = 56 * 1024 * 1024
BF16 = jnp.bfloat16
F32 = jnp.float32


def _cparams(*sem):
    return pltpu.CompilerParams(dimension_semantics=sem, vmem_limit_bytes=VMEM_LIMIT)


def _resident(shape):
    nd = len(shape)
    return pl.BlockSpec(shape, lambda *_: (0,) * nd, pipeline_mode=pl.Buffered(1))


def _rms(x, g):
    return x * lax.rsqrt(jnp.mean(x * x, axis=-1, keepdims=True) + EPS) * g


def _dot(a, b):
    return jnp.dot(a, b, preferred_element_type=F32)


def _dot_nt(a, b):
    return lax.dot_general(a, b, (((1,), (1,)), ((), ())), preferred_element_type=F32)


def _ffn_body(xp_ref, xs_ref, g_ref, wg_ref, wu_ref, wd_ref, fg_ref, op_ref, os_ref, *, chunk, final, n_prompt):
    def half_step(x_ref, o_ref):
        x = x_ref[...]
        h = _rms(x, g_ref[...]).astype(BF16)
        d_ff = wg_ref.shape[1]
        acc = jnp.zeros(x.shape, F32)
        for c in range(d_ff // chunk):
            sl = slice(c * chunk, (c + 1) * chunk)
            gate = _dot(h, wg_ref[:, sl].astype(BF16))
            up = _dot(h, wu_ref[:, sl].astype(BF16))
            act = (gate * jax.nn.sigmoid(gate) * up).astype(BF16)
            acc = acc + _dot(act, wd_ref[sl, :].astype(BF16))
        y = x + 0.5 * acc
        if final:
            y = _rms(y, fg_ref[...])
        o_ref[...] = y

    i = pl.program_id(0)

    @pl.when(i < n_prompt)
    def _():
        half_step(xp_ref, op_ref)

    @pl.when(i == n_prompt)
    def _():
        half_step(xs_ref, os_ref)


def _ffn(xp, xs, g, wg, wu, wd, fg, layer, *, final=False):
    n, d = xp.shape
    ns = xs.shape[0]
    d_ff = wg.shape[2]
    tm = min(512, n)
    n_prompt = n // tm
    chunk = 256 if d_ff % 256 == 0 else LANES
    prompt_row = lambda i: (jnp.minimum(i, n_prompt - 1), 0)
    whole = lambda i: (0, 0)
    layer_block = lambda shape: pl.BlockSpec((None,) + shape, lambda i: (layer, 0, 0), pipeline_mode=pl.Buffered(1))
    return pl.pallas_call(
        functools.partial(_ffn_body, chunk=chunk, final=final, n_prompt=n_prompt),
        out_shape=[jax.ShapeDtypeStruct((n, d), F32), jax.ShapeDtypeStruct((ns, d), F32)],
        grid=(n_prompt + 1,),
        in_specs=[pl.BlockSpec((tm, d), prompt_row), _resident((ns, d)), layer_block((1, d)),
                  layer_block((d, d_ff)), layer_block((d, d_ff)), layer_block((d_ff, d)), _resident((1, d))],
        out_specs=[pl.BlockSpec((tm, d), prompt_row), pl.BlockSpec((ns, d), whole)],
        compiler_params=_cparams("arbitrary"),
        name="ffn",
    )(xp, xs, g.reshape(g.shape[0], 1, d), wg, wu, wd, fg)


def _rope(v, t1, t2):
    lane = lax.broadcasted_iota(jnp.int32, v.shape, 1)
    partner = jnp.where(lane < 80, pltpu.roll(v, 112, 1), pltpu.roll(v, 16, 1))
    return v * t1 + partner * t2


def _proj_body(x_ref, t1_ref, t2_ref, g_ref, wq_ref, wkv_ref, wkr_ref, wa_ref, wgl_ref, qn_ref, kvn_ref,
               wuq_ref, wk_ref, *rest, absorbed, n_heads):
    if absorbed:
        q_ref, lat_ref, kr_ref, u_ref, qa_ref = rest
    else:
        wv_ref, q_ref, lat_ref, kr_ref, u_ref, k_ref, v_ref = rest
    x = x_ref[...]
    h = _rms(x, g_ref[...]).astype(BF16)
    t1 = t1_ref[...]
    t2 = t2_ref[...]

    u_ref[...] = _dot(h, wa_ref[...]) * jax.nn.sigmoid(_dot(h, wgl_ref[...]))

    c_kv = _rms(_dot(h, wkv_ref[...]), kvn_ref[...])
    lat_ref[...] = c_kv
    c_bf = c_kv.astype(BF16)

    kr = _rope(_dot(h, wkr_ref[...]), t1, t2)
    kr_ref[...] = kr[:, 64:96]

    qn = _rms(_dot(h, wq_ref[...]), qn_ref[...]).astype(BF16)
    q = _dot(qn, wuq_ref[...])
    q_heads = [_rope(q[:, i * HEAD_PAD:(i + 1) * HEAD_PAD], t1, t2) for i in range(n_heads)]

    if absorbed:
        q_heads = [qh.astype(BF16) for qh in q_heads]
        q_ref[...] = jnp.concatenate(q_heads, axis=1)
        kv = wk_ref.shape[2]
        for i in range(n_heads):
            qa_ref[:, i * kv:(i + 1) * kv] = _dot(q_heads[i], wk_ref[i]).astype(BF16)
    else:
        q_ref[0] = jnp.concatenate(q_heads, axis=1).T.astype(BF16)
        k = _dot(c_bf, wk_ref[...])
        k_ref[...] = jnp.concatenate(
            [(k[:, i * HEAD_PAD:(i + 1) * HEAD_PAD] + kr).astype(BF16) for i in range(n_heads)], axis=1)
        v_ref[0] = _dot(c_bf, wv_ref[...]).T.astype(BF16)


def _proj(x, t1, t2, w, *, absorbed, seqs=1):
    n, d = x.shape
    tm = min(512, n // seqs)
    n_heads = w["n_heads"]
    kv = w["w_kv"].shape[1]
    rope = w["rope"]
    conv_ch = w["w_a"].shape[1]
    row = lambda i: (i, 0)
    rows = lambda c, dt: (jax.ShapeDtypeStruct((n, c), dt), pl.BlockSpec((tm, c), row))
    per_seq = (n // seqs) // tm

    def transposed(c):
        return (jax.ShapeDtypeStruct((seqs, c, n // seqs), BF16),
                pl.BlockSpec((1, c, tm), lambda i: (i // per_seq, 0, i % per_seq)))

    weights = [w["mix_norm"], w["w_q"], w["w_kv"], w["w_kr"], w["w_a"], w["w_g"], w["q_norm"], w["kv_norm"],
               w["w_uq"]] + ([w["w_uk_abs"]] if absorbed else [w["w_uk_pad"], w["w_uv"]])
    common = [rows(kv, F32), rows(rope, F32), rows(conv_ch, F32)]
    if absorbed:
        outs = [rows(n_heads * HEAD_PAD, BF16)] + common + [rows(n_heads * kv, BF16)]
    else:
        outs = [transposed(n_heads * HEAD_PAD)] + common + [rows(n_heads * HEAD_PAD, BF16),
                                                            transposed(w["w_uv"].shape[1])]
    return pl.pallas_call(
        functools.partial(_proj_body, absorbed=absorbed, n_heads=n_heads),
        out_shape=[o[0] for o in outs],
        grid=(n // tm,),
        in_specs=[pl.BlockSpec((tm, d), row), pl.BlockSpec((tm, LANES), row), pl.BlockSpec((tm, LANES), row)]
                 + [_resident(a.shape) for a in weights],
        out_specs=[o[1] for o in outs],
        compiler_params=_cparams("parallel"),
        name="proj_abs" if absorbed else "proj",
    )(x, t1, t2, *weights)


def _attn_body(qt_ref, k_ref, vt_ref, o_ref, *, tq, v_dim, unroll):
    qi = pl.program_id(2)
    heads = (0, 1)
    qts = [qt_ref[0, hh * HEAD_PAD:(hh + 1) * HEAD_PAD, :] for hh in heads]

    def chunk(j):
        return pl.ds(pl.multiple_of(j * tq, tq), tq)

    def scores(j):
        return tuple(_dot(k_ref[0, chunk(j), hh * HEAD_PAD:(hh + 1) * HEAD_PAD], qts[hh]) for hh in heads)

    def softmax(sts, ms, ls, masked):
        ps, ms_new, ls_new, alphas = [], [], [], []
        for hh in heads:
            st = sts[hh]
            if masked:
                key = lax.broadcasted_iota(jnp.int32, st.shape, 0)
                qry = lax.broadcasted_iota(jnp.int32, st.shape, 1)
                st = jnp.where(key <= qry, st, NEG_BIG)
            m_new = jnp.maximum(ms[hh], jnp.max(st, axis=0, keepdims=True))
            p = jnp.exp2(st - m_new)
            alpha = jnp.exp2(ms[hh] - m_new)
            ps.append(p.astype(BF16))
            ms_new.append(m_new)
            ls_new.append(alpha * ls[hh] + jnp.sum(p, axis=0, keepdims=True))
            alphas.append(alpha)
        return tuple(ps), tuple(ms_new), tuple(ls_new), tuple(alphas)

    def accumulate(j, ps, alphas, accs):
        return tuple(alphas[hh] * accs[hh] + _dot(vt_ref[0, hh * v_dim:(hh + 1) * v_dim, chunk(j)], ps[hh])
                     for hh in heads)

    def group(j, carry, n, last_masked):
        ms, ls, accs = carry
        sts = [scores(j + i) for i in range(n)]
        for i in range(n):
            ps, ms, ls, alphas = softmax(sts[i], ms, ls, last_masked and i == n - 1)
            accs = accumulate(j + i, ps, alphas, accs)
        return ms, ls, accs

    both = lambda x: (x, x)
    carry = (both(jnp.full((1, tq), NEG_BIG, F32)), both(jnp.zeros((1, tq), F32)), both(jnp.zeros((v_dim, tq), F32)))
    carry = lax.fori_loop(0, qi // unroll, lambda jj, c: group(unroll * jj, c, unroll, False), carry)
    rest = qi % unroll
    tails = [functools.partial(group, n=r + 1, last_masked=True) for r in range(unroll)]
    _, ls, accs = lax.switch(rest, [lambda c, t=t: t(qi - rest, c) for t in tails], carry)
    o_ref[0] = jnp.concatenate([accs[0] / ls[0], accs[1] / ls[1]], axis=0).T.astype(o_ref.dtype)


def _attn_prompt(qt, k, vt, *, n_heads, v_dim):
    b, t, _ = k.shape
    tq = min(512, t)
    return pl.pallas_call(
        functools.partial(_attn_body, tq=tq, v_dim=v_dim, unroll=4),
        out_shape=jax.ShapeDtypeStruct((b, t, n_heads * v_dim), BF16),
        grid=(b, n_heads // 2, t // tq),
        in_specs=[pl.BlockSpec((1, 2 * HEAD_PAD, tq), lambda bi, hp, qi: (bi, hp, qi)),
                  pl.BlockSpec((1, t, 2 * HEAD_PAD), lambda bi, hp, qi: (bi, 0, hp)),
                  pl.BlockSpec((1, 2 * v_dim, t), lambda bi, hp, qi: (bi, hp, 0))],
        out_specs=pl.BlockSpec((1, tq, 2 * v_dim), lambda bi, hp, qi: (bi, qi, hp)),
        compiler_params=_cparams("parallel", "parallel", "arbitrary"),
        name="attn_prompt",
    )(qt, k, vt)


def _paged_body(pt_ref, qa_ref, qr_ref, cn_ref, kn_ref, lat_hbm, krt_hbm, o_ref, lat_buf, krt_buf, sems,
                *, e, n_pages, n_heads):
    b = pl.program_id(0)
    page = krt_hbm.shape[3]

    def page_copies(seq, slot):
        copies = []
        for pi in range(n_pages):
            pid = pt_ref[seq * n_pages + pi]
            rows = pl.ds(pi * page, page)
            copies.append(pltpu.make_async_copy(lat_hbm.at[e, pid], lat_buf.at[slot, rows], sems.at[0, slot]))
            copies.append(pltpu.make_async_copy(krt_hbm.at[e, pid], krt_buf.at[slot, :, rows], sems.at[1, slot]))
        return copies

    @pl.when(b == 0)
    def _():
        for cp in page_copies(0, 0):
            cp.start()

    @pl.when(b + 1 < pl.num_programs(0))
    def _():
        for cp in page_copies(b + 1, (b + 1) % 2):
            cp.start()

    slot = b % 2
    for cp in page_copies(b, slot):
        cp.wait()

    qa = qa_ref[0]
    qr = qr_ref[0]
    lat = lat_buf[slot].astype(BF16)
    krt = krt_buf[slot].astype(BF16)
    cn = cn_ref[0].astype(BF16)
    kn = kn_ref[0].astype(BF16)
    s = _dot_nt(qa, lat) + _dot(qr, krt)
    sn = _dot_nt(qa, cn) + _dot_nt(qr, kn)
    r = lax.broadcasted_iota(jnp.int32, sn.shape, 0) // n_heads
    c = lax.broadcasted_iota(jnp.int32, sn.shape, 1)
    sn = jnp.where(c <= r, sn, NEG_BIG)
    m = jnp.maximum(jnp.max(s, axis=1, keepdims=True), jnp.max(sn, axis=1, keepdims=True))
    p = jnp.exp2(s - m)
    pn = jnp.exp2(sn - m)
    l = jnp.sum(p, axis=1, keepdims=True) + jnp.sum(pn, axis=1, keepdims=True)
    o = _dot(p.astype(BF16), lat) + _dot(pn.astype(BF16), cn)
    o_ref[0] = (o / l).astype(o_ref.dtype)


def _attn_paged(e, page_table, qa, qr, c_new, k_new, cache_latent, cache_krope_t, *, n_heads):
    db, rows, kv = qa.shape
    rope = qr.shape[2]
    n_pages = page_table.shape[1]
    page = cache_latent.shape[2]
    past = n_pages * page
    new_pad = c_new.shape[1]
    per_b = lambda b, pt: (b, 0, 0)
    grid_spec = pltpu.PrefetchScalarGridSpec(
        num_scalar_prefetch=1,
        grid=(db,),
        in_specs=[pl.BlockSpec((1, rows, kv), per_b), pl.BlockSpec((1, rows, rope), per_b),
                  pl.BlockSpec((1, new_pad, kv), per_b), pl.BlockSpec((1, new_pad, rope), per_b),
                  pl.BlockSpec(memory_space=pl.ANY), pl.BlockSpec(memory_space=pl.ANY)],
        out_specs=pl.BlockSpec((1, rows, kv), per_b),
        scratch_shapes=[pltpu.VMEM((2, past, kv), F32), pltpu.VMEM((2, rope, past), F32),
                        pltpu.SemaphoreType.DMA((2, 2))],
    )
    return pl.pallas_call(
        functools.partial(_paged_body, e=e, n_pages=n_pages, n_heads=n_heads),
        out_shape=jax.ShapeDtypeStruct((db, rows, kv), BF16),
        grid_spec=grid_spec,
        compiler_params=_cparams("arbitrary"),
        name="attn_paged",
    )(page_table.reshape(-1), qa, qr, c_new, k_new, cache_latent, cache_krope_t)


def _uv_body(o_ref, w_ref, a_ref, *, n_heads):
    a_ref[...] = jnp.concatenate([_dot(o_ref[i], w_ref[i]) for i in range(n_heads)], axis=1).astype(a_ref.dtype)


def _uv_proj(o_hm, w_uv_h):
    n_heads, n, kv = o_hm.shape
    v_dim = w_uv_h.shape[2]
    return pl.pallas_call(
        functools.partial(_uv_body, n_heads=n_heads),
        out_shape=jax.ShapeDtypeStruct((n, n_heads * v_dim), BF16),
        grid=(1,),
        in_specs=[_resident(o_hm.shape), _resident(w_uv_h.shape)],
        out_specs=pl.BlockSpec((n, n_heads * v_dim), lambda i: (0, 0)),
        compiler_params=_cparams("arbitrary"),
        name="uv_proj",
    )(o_hm, w_uv_h)


def _conv_body(x_ref, att_ref, u_ref, hist_ref, cw_ref, cb_ref, lg_ref, lb_ref, woa_ref, woc_ref,
               o_ref, st_ref, win_ref, *shift_refs, stride, width, sub):
    ti = pl.program_id(1)
    nt = pl.num_programs(1)
    tt = u_ref.shape[1]
    halo = hist_ref.shape[1]
    pad = halo // stride - (width - 1)

    @pl.when(ti == 0)
    def _():
        win_ref[0:halo, :] = hist_ref[0]

    win_ref[halo:halo + tt, :] = u_ref[0]

    offsets = [(kk + pad) * stride for kk in range(width)]
    shifts = _conv_shifts(offsets)
    for sh, ref in zip(shifts, shift_refs):
        ref[...] = win_ref[sh:sh + ref.shape[0], :]

    def tap(off, r0):
        sh = off % SUBLANES
        src = win_ref if sh == 0 else shift_refs[shifts.index(sh)]
        return src[r0 + off - sh:r0 + off - sh + sub, :]

    cb = cb_ref[...]
    lg = lg_ref[...]
    lb = lb_ref[...]
    cvs = []
    for r0 in range(0, tt, sub):
        y = jnp.zeros((sub, u_ref.shape[2]), F32)
        for kk in range(width):
            y = y + cw_ref[kk:kk + 1, :] * tap(offsets[kk], r0)
        y = y + cb
        yc = y - jnp.mean(y, axis=-1, keepdims=True)
        var = jnp.mean(yc * yc, axis=-1, keepdims=True)
        z = yc * lax.rsqrt(var + EPS) * lg + lb
        cvs.append((z * jax.nn.sigmoid(z)).astype(BF16))
    cv = jnp.concatenate(cvs, axis=0)
    o_ref[0] = x_ref[0] + _dot(att_ref[0], woa_ref[...]) + _dot(cv, woc_ref[...])

    @pl.when(ti == nt - 1)
    def _():
        st_ref[0] = win_ref[tt + pad * stride:tt + halo, :]

    if halo <= tt:
        @pl.when(ti < nt - 1)
        def _():
            win_ref[0:halo, :] = win_ref[tt:tt + halo, :]


def _conv_shifts(offsets):
    return sorted({off % SUBLANES for off in offsets} - {0})


def _conv_mix(x, att, u, hist, w, *, stride):
    bo, t, d = x.shape
    c = u.shape[2]
    a = att.shape[2]
    halo = hist.shape[1]
    width = w["conv_w"].shape[0]
    tt = min(512, t)
    assert t == tt or halo <= tt
    keep = (width - 1) * stride
    pad = halo // stride - (width - 1)
    n_shift = len(_conv_shifts([(kk + pad) * stride for kk in range(width)]))
    tile = lambda b, i: (b, i, 0)
    per_b = lambda b, i: (b, 0, 0)
    weights = [w["conv_w"], w["conv_b"], w["conv_ln_g"], w["conv_ln_b"], w["w_out_att"], w["w_out_conv"]]
    return pl.pallas_call(
        functools.partial(_conv_body, stride=stride, width=width, sub=min(64, tt)),
        out_shape=[jax.ShapeDtypeStruct((bo, t, d), F32), jax.ShapeDtypeStruct((bo, keep, c), F32)],
        grid=(bo, t // tt),
        in_specs=[pl.BlockSpec((1, tt, d), tile), pl.BlockSpec((1, tt, a), tile), pl.BlockSpec((1, tt, c), tile),
                  pl.BlockSpec((1, halo, c), per_b)] + [_resident(v.shape) for v in weights],
        out_specs=[pl.BlockSpec((1, tt, d), tile), pl.BlockSpec((1, keep, c), per_b)],
        scratch_shapes=[pltpu.VMEM((halo + tt, c), F32)] + [pltpu.VMEM((halo + tt - SUBLANES, c), F32)] * n_shift,
        compiler_params=_cparams("parallel", "arbitrary"),
        name="conv_mix",
    )(x, att, u, hist, *weights)


def _pool_body(x_ref, hist_ref, g_ref, pw_ref, ps_ref, o_ref, st_ref, win_ref, lvl_ref, *, stride, windows, pos0):
    ti = pl.program_id(1)
    nt = pl.num_programs(1)
    tt = x_ref.shape[1]
    halo = hist_ref.shape[1]
    d = x_ref.shape[2]
    grp = d // len(windows)

    @pl.when(ti == 0)
    def _():
        win_ref[0:halo, :] = hist_ref[0]

    x = x_ref[0]
    h = _rms(x, g_ref[...])
    win_ref[halo:halo + tt, :] = h

    pos = pos0 + ti * (tt // stride) + lax.broadcasted_iota(jnp.int32, (tt, 1), 0) // stride
    ys = []
    for gi, wn in enumerate(windows):
        cols = slice(gi * grp, (gi + 1) * grp)
        base = halo - (wn - 1) * stride
        n = tt + (wn - 2) * stride
        lvl_ref[0:n, cols] = win_ref[base + stride:base + stride + n, cols] + win_ref[base:base + n, cols]
        k = 2
        while k < wn:
            n -= k * stride
            lvl_ref[0:n, cols] = lvl_ref[k * stride:k * stride + n, cols] + lvl_ref[0:n, cols]
            k *= 2
        s = lvl_ref[0:tt, cols]
        cnt = jnp.minimum(pos + 1, wn).astype(F32)
        diff = (s / cnt - h[:, cols]).astype(BF16)
        ys.append(_dot(diff, pw_ref[gi]))
    o_ref[0] = x + jnp.concatenate(ys, axis=1) * ps_ref[...]

    keep = st_ref.shape[1]

    @pl.when(ti == nt - 1)
    def _():
        st_ref[0] = win_ref[tt + halo - keep:tt + halo, :]

    if halo <= tt:
        @pl.when(ti < nt - 1)
        def _():
            win_ref[0:halo, :] = win_ref[tt:tt + halo, :]


def _pool_mix(x, hist, g, pool_w, pool_scale, *, stride, windows, pos0):
    bo, t, d = x.shape
    halo = hist.shape[1]
    tt = min(512, t)
    assert t == tt or halo <= tt
    keep = (max(windows) - 1) * stride
    tile = lambda b, i: (b, i, 0)
    per_b = lambda b, i: (b, 0, 0)
    return pl.pallas_call(
        functools.partial(_pool_body, stride=stride, windows=windows, pos0=pos0),
        out_shape=[jax.ShapeDtypeStruct((bo, t, d), F32), jax.ShapeDtypeStruct((bo, keep, d), F32)],
        grid=(bo, t // tt),
        in_specs=[pl.BlockSpec((1, tt, d), tile), pl.BlockSpec((1, halo, d), per_b),
                  _resident(g.shape), _resident(pool_w.shape), _resident(pool_scale.shape)],
        out_specs=[pl.BlockSpec((1, tt, d), tile), pl.BlockSpec((1, keep, d), per_b)],
        scratch_shapes=[pltpu.VMEM((halo + tt, d), F32), pltpu.VMEM((halo + tt, d), F32)],
        compiler_params=_cparams("parallel", "arbitrary"),
        name="pool_mix",
    )(x, hist, g, pool_w, pool_scale)


def _rope_tables(pos, nope, rope):
    inv = ROPE_BASE ** (-jnp.arange(0, rope, 2, dtype=F32) / rope)
    ang = pos[:, None] * inv[None, :]
    cos, sin = jnp.cos(ang), jnp.sin(ang)
    n = pos.shape[0]
    zero_hi = jnp.zeros((n, HEAD_PAD - nope - rope), F32)
    t1 = jnp.concatenate([jnp.ones((n, nope), F32), cos, cos, zero_hi], axis=1)
    t2 = jnp.concatenate([jnp.zeros((n, nope), F32), -sin, sin, zero_hi], axis=1)
    return t1, t2


def _time_major(a):
    b, s, c = a.shape
    return a.transpose(1, 0, 2).reshape(1, s * b, c)


def _batch_major(a, b):
    c = a.shape[2]
    return a.reshape(-1, b, c).transpose(1, 0, 2)


def kernel(x_prompt, x_sample, cache_latent, cache_krope, state_conv, state_pool, page_table, ffn1_norm, ffn1_w_gate, ffn1_w_up, ffn1_w_down, mix_norm, w_in, q_norm, w_uq, kv_norm, w_uk, w_uv, conv_w, conv_b, conv_ln_g, conv_ln_b, w_out, pool_w, pool_scale, ffn2_norm, ffn2_w_gate, ffn2_w_up, ffn2_w_down, final_norm):
    B, T, D = x_prompt.shape
    DB, TS, _ = x_sample.shape
    depth = ffn1_norm.shape[0]
    q_lora, n_heads, qk_dim = w_uq.shape[1:]
    kv_lora, _, nope = w_uk.shape[1:]
    v_dim = w_uv.shape[3]
    rope = qk_dim - nope
    conv_ch = conv_w.shape[2]
    conv_width = conv_w.shape[1]
    n_groups = pool_w.shape[1]
    grp = D // n_groups
    windows = tuple(2 ** (i + 1) for i in range(n_groups))
    pool_state = state_pool.shape[2]
    n_pages = page_table.shape[1]
    page = cache_latent.shape[2]
    past_len = n_pages * page
    scale = float(qk_dim) ** -0.5 * LOG2_E
    assert nope == 64 and rope == 32 and v_dim == 64 and max(windows) - 1 == pool_state and grp * n_groups == D

    t1_p, t2_p = _rope_tables(jnp.arange(T, dtype=F32), nope, rope)
    t1_p, t2_p = jnp.tile(t1_p, (B, 1)), jnp.tile(t2_p, (B, 1))
    t1_s, t2_s = _rope_tables(past_len + jnp.arange(TS, dtype=F32), nope, rope)
    t1_s, t2_s = jnp.tile(t1_s, (DB, 1)), jnp.tile(t2_s, (DB, 1))

    def pad_rope_cols(wr, lead):
        z0 = jnp.zeros(wr.shape[:-1] + (lead,), wr.dtype)
        z1 = jnp.zeros(wr.shape[:-1] + (HEAD_PAD - lead - rope,), wr.dtype)
        return jnp.concatenate([z0, wr, z1], axis=-1)

    def even_weights(l):
        e = l // 2
        wi = w_in[e]
        o = 0
        w_q = wi[:, o:o + q_lora]; o += q_lora
        w_kv = wi[:, o:o + kv_lora]; o += kv_lora
        w_kr = wi[:, o:o + rope]; o += rope
        w_a = wi[:, o:o + conv_ch]; o += conv_ch
        w_g = wi[:, o:o + conv_ch]
        uq = w_uq[e] * scale
        uq = jnp.concatenate([uq, jnp.zeros((q_lora, n_heads, HEAD_PAD - qk_dim), F32)], axis=-1)
        uk = w_uk[e]
        uk_pad = jnp.concatenate([uk, jnp.zeros((kv_lora, n_heads, HEAD_PAD - nope), F32)], axis=-1)
        uk_abs = jnp.concatenate([uk.transpose(1, 2, 0),
                                  jnp.zeros((n_heads, HEAD_PAD - nope, kv_lora), F32)], axis=1)
        return dict(
            n_heads=n_heads, rope=rope,
            mix_norm=mix_norm[l][None], w_q=w_q.astype(BF16), w_kv=w_kv.astype(BF16),
            w_kr=pad_rope_cols(w_kr, nope).astype(BF16), w_a=w_a.astype(BF16), w_g=w_g.astype(BF16),
            q_norm=q_norm[e][None], kv_norm=kv_norm[e][None],
            w_uq=uq.reshape(q_lora, n_heads * HEAD_PAD).astype(BF16),
            w_uk_pad=uk_pad.reshape(kv_lora, n_heads * HEAD_PAD).astype(BF16),
            w_uk_abs=uk_abs.astype(BF16),
            w_uv=w_uv[e].reshape(kv_lora, n_heads * v_dim).astype(BF16),
            w_uv_h=w_uv[e].transpose(1, 0, 2).astype(BF16),
            conv_w=conv_w[e], conv_b=conv_b[e][None], conv_ln_g=conv_ln_g[e][None], conv_ln_b=conv_ln_b[e][None],
            w_out_att=w_out[e][:n_heads * v_dim].astype(BF16), w_out_conv=w_out[e][n_heads * v_dim:].astype(BF16),
        )

    cache_krope_t = jnp.swapaxes(cache_krope, 2, 3)
    xp = x_prompt.reshape(B * T, D)
    xs = x_sample.reshape(DB * TS, D)
    lat_p, kr_p, lat_s, kr_s, conv_p, conv_s, pool_p, pool_s = ([] for _ in range(8))
    conv_halo = 32 if conv_width - 1 <= 32 else conv_width - 1
    pool_halo = pool_state + 1

    for l in range(depth):
        fg = final_norm[None]
        w1 = (ffn1_norm, ffn1_w_gate, ffn1_w_up, ffn1_w_down)
        xp, xs = _ffn(xp, xs, *w1, fg, l)
        if l % 2 == 0:
            e = l // 2
            w = even_weights(l)
            qt_p, c_p, k_rope_p, u_p, k_p, vt_p = _proj(xp, t1_p, t2_p, w, absorbed=False, seqs=B)
            q_s, c_s, k_rope_s, u_s, qa_s = _proj(xs, t1_s, t2_s, w, absorbed=True)

            att_p = _attn_prompt(qt_p, k_p.reshape(B, T, -1), vt_p, n_heads=n_heads, v_dim=v_dim)

            qa = qa_s.reshape(DB, TS * n_heads, kv_lora)
            qr = q_s.reshape(DB, TS * n_heads, HEAD_PAD)[:, :, nope:nope + rope]
            new_pad = 16
            c_new = jnp.pad(c_s.reshape(DB, TS, kv_lora), ((0, 0), (0, new_pad - TS), (0, 0)))
            k_new = jnp.pad(k_rope_s.reshape(DB, TS, rope), ((0, 0), (0, new_pad - TS), (0, 0)))
            o_s = _attn_paged(e, page_table, qa, qr, c_new, k_new, cache_latent, cache_krope_t, n_heads=n_heads)
            o_hm = o_s.reshape(DB * TS, n_heads, kv_lora).transpose(1, 0, 2)
            att_s = _uv_proj(o_hm, w["w_uv_h"])

            xp3, st_p = _conv_mix(xp.reshape(B, T, D), att_p, u_p.reshape(B, T, conv_ch),
                                  jnp.zeros((B, conv_halo, conv_ch), F32), w, stride=1)
            xp = xp3.reshape(B * T, D)
            hist_s = jnp.pad(state_conv[e], ((0, 0), (conv_halo - (conv_width - 1), 0), (0, 0)))
            xs3, st_s = _conv_mix(_time_major(xs.reshape(DB, TS, D)), _time_major(att_s.reshape(DB, TS, -1)),
                                  _time_major(u_s.reshape(DB, TS, conv_ch)), _time_major(hist_s), w, stride=DB)
            xs = _batch_major(xs3, DB).reshape(DB * TS, D)

            lat_p.append(c_p.reshape(B, T, kv_lora)); kr_p.append(k_rope_p.reshape(B, T, rope))
            lat_s.append(c_s.reshape(DB, TS, kv_lora)); kr_s.append(k_rope_s.reshape(DB, TS, rope))
            conv_p.append(st_p); conv_s.append(_batch_major(st_s, DB))
        else:
            o = l // 2
            g = mix_norm[l][None]
            pw = pool_w[o].astype(BF16)
            ps = pool_scale[o][None]
            xp3, st_p = _pool_mix(xp.reshape(B, T, D), jnp.zeros((B, pool_halo, D), F32), g, pw, ps,
                                  stride=1, windows=windows, pos0=0)
            xp = xp3.reshape(B * T, D)
            hist_s = jnp.pad(state_pool[o], ((0, 0), (pool_halo - pool_state, 0), (0, 0)))
            xs3, st_s = _pool_mix(_time_major(xs.reshape(DB, TS, D)), _time_major(hist_s), g, pw, ps,
                                  stride=DB, windows=windows, pos0=past_len)
            xs = _batch_major(xs3, DB).reshape(DB * TS, D)
            pool_p.append(st_p); pool_s.append(_batch_major(st_s, DB))
        w2 = (ffn2_norm, ffn2_w_gate, ffn2_w_up, ffn2_w_down)
        last = l == depth - 1
        xp, xs = _ffn(xp, xs, *w2, fg, l, final=last)

    return (xp.reshape(B, T, D), xs.reshape(DB, TS, D),
            jnp.stack(lat_p), jnp.stack(kr_p), jnp.stack(lat_s), jnp.stack(kr_s),
            jnp.stack(conv_p), jnp.stack(conv_s), jnp.stack(pool_p), jnp.stack(pool_s))
```

```python
import functools

import jax
import jax.numpy as jnp
from jax import lax
from jax.experimental import pallas as pl
from jax.experimental.pallas import tpu as pltpu

EPS = 1e-6
ROPE_BASE = 10000.0
LANES = 128
SUBLANES = 8
HEAD_PAD = 128
NEG_BIG = -1e30
LOG2_E = 1.4426950408889634
VMEM_LIMIT = 56 * 1024 * 1024
BF16 = jnp.bfloat16
F32 = jnp.float32


def _cparams(*sem):
    return pltpu.CompilerParams(dimension_semantics=sem, vmem_limit_bytes=VMEM_LIMIT)


def _resident(shape):
    nd = len(shape)
    return pl.BlockSpec(shape, lambda *_: (0,) * nd, pipeline_mode=pl.Buffered(1))


def _rms(x, g):
    return x * lax.rsqrt(jnp.mean(x * x, axis=-1, keepdims=True) + EPS) * g


def _dot(a, b):
    return jnp.dot(a, b, preferred_element_type=F32)


def _dot_nt(a, b):
    return lax.dot_general(a, b, (((1,), (1,)), ((), ())), preferred_element_type=F32)


def _ffn_body(xp_ref, xs_ref, g_ref, wg_ref, wu_ref, wd_ref, fg_ref, op_ref, os_ref, *, chunk, final, n_prompt):
    def half_step(x_ref, o_ref):
        x = x_ref[...]
        h = _rms(x, g_ref[...]).astype(BF16)
        d_ff = wg_ref.shape[1]
        acc = jnp.zeros(x.shape, F32)
        for c in range(d_ff // chunk):
            sl = slice(c * chunk, (c + 1) * chunk)
            gate = _dot(h, wg_ref[:, sl].astype(BF16))
            up = _dot(h, wu_ref[:, sl].astype(BF16))
            act = (gate * jax.nn.sigmoid(gate) * up).astype(BF16)
            acc = acc + _dot(act, wd_ref[sl, :].astype(BF16))
        y = x + 0.5 * acc
        if final:
            y = _rms(y, fg_ref[...])
        o_ref[...] = y

    i = pl.program_id(0)

    @pl.when(i < n_prompt)
    def _():
        half_step(xp_ref, op_ref)

    @pl.when(i == n_prompt)
    def _():
        half_step(xs_ref, os_ref)


def _ffn(xp, xs, g, wg, wu, wd, fg, layer, *, final=False):
    n, d = xp.shape
    ns = xs.shape[0]
    d_ff = wg.shape[2]
    tm = min(512, n)
    n_prompt = n // tm
    chunk = 256 if d_ff % 256 == 0 else LANES
    prompt_row = lambda i: (jnp.minimum(i, n_prompt - 1), 0)
    whole = lambda i: (0, 0)
    layer_block = lambda shape: pl.BlockSpec((None,) + shape, lambda i: (layer, 0, 0), pipeline_mode=pl.Buffered(1))
    return pl.pallas_call(
        functools.partial(_ffn_body, chunk=chunk, final=final, n_prompt=n_prompt),
        out_shape=[jax.ShapeDtypeStruct((n, d), F32), jax.ShapeDtypeStruct((ns, d), F32)],
        grid=(n_prompt + 1,),
        in_specs=[pl.BlockSpec((tm, d), prompt_row), _resident((ns, d)), layer_block((1, d)),
                  layer_block((d, d_ff)), layer_block((d, d_ff)), layer_block((d_ff, d)), _resident((1, d))],
        out_specs=[pl.BlockSpec((tm, d), prompt_row), pl.BlockSpec((ns, d), whole)],
        compiler_params=_cparams("arbitrary"),
        name="ffn",
    )(xp, xs, g.reshape(g.shape[0], 1, d), wg, wu, wd, fg)


def _rope(v, t1, t2):
    lane = lax.broadcasted_iota(jnp.int32, v.shape, 1)
    partner = jnp.where(lane < 80, pltpu.roll(v, 112, 1), pltpu.roll(v, 16, 1))
    return v * t1 + partner * t2


def _proj_body(x_ref, t1_ref, t2_ref, g_ref, wq_ref, wkv_ref, wkr_ref, wa_ref, wgl_ref, qn_ref, kvn_ref,
               wuq_ref, wk_ref, *rest, absorbed, n_heads):
    if absorbed:
        q_ref, lat_ref, kr_ref, u_ref, qa_ref = rest
    else:
        wv_ref, q_ref, lat_ref, kr_ref, u_ref, k_ref, v_ref = rest
    x = x_ref[...]
    h = _rms(x, g_ref[...]).astype(BF16)
    t1 = t1_ref[...]
    t2 = t2_ref[...]

    qn = _rms(_dot(h, wq_ref[...]), qn_ref[...]).astype(BF16)

    c_kv = _rms(_dot(h, wkv_ref[...]), kvn_ref[...])
    lat_ref[...] = c_kv
    c_bf = c_kv.astype(BF16)

    q = _dot(qn, wuq_ref[...])

    kr = _rope(_dot(h, wkr_ref[...]), t1, t2)
    kr_ref[...] = kr[:, 64:96]

    u_ref[...] = _dot(h, wa_ref[...]) * jax.nn.sigmoid(_dot(h, wgl_ref[...]))
    q_heads = [_rope(q[:, i * HEAD_PAD:(i + 1) * HEAD_PAD], t1, t2) for i in range(n_heads)]

    if absorbed:
        q_heads = [qh.astype(BF16) for qh in q_heads]
        q_ref[...] = jnp.concatenate(q_heads, axis=1)
        kv = wk_ref.shape[2]
        for i in range(n_heads):
            qa_ref[:, i * kv:(i + 1) * kv] = _dot(q_heads[i], wk_ref[i]).astype(BF16)
    else:
        q_ref[0] = jnp.concatenate(q_heads, axis=1).T.astype(BF16)
        k = _dot(c_bf, wk_ref[...])
        k_ref[...] = jnp.concatenate(
            [(k[:, i * HEAD_PAD:(i + 1) * HEAD_PAD] + kr).astype(BF16) for i in range(n_heads)], axis=1)
        v_ref[0] = _dot(c_bf, wv_ref[...]).T.astype(BF16)


def _proj(x, t1, t2, w, *, absorbed, seqs=1):
    n, d = x.shape
    tm = min(512, n // seqs)
    n_heads = w["n_heads"]
    kv = w["w_kv"].shape[1]
    rope = w["rope"]
    conv_ch = w["w_a"].shape[1]
    row = lambda i: (i, 0)
    rows = lambda c, dt: (jax.ShapeDtypeStruct((n, c), dt), pl.BlockSpec((tm, c), row))
    per_seq = (n // seqs) // tm

    def transposed(c):
        return (jax.ShapeDtypeStruct((seqs, c, n // seqs), BF16),
                pl.BlockSpec((1, c, tm), lambda i: (i // per_seq, 0, i % per_seq)))

    weights = [w["mix_norm"], w["w_q"], w["w_kv"], w["w_kr"], w["w_a"], w["w_g"], w["q_norm"], w["kv_norm"],
               w["w_uq"]] + ([w["w_uk_abs"]] if absorbed else [w["w_uk_pad"], w["w_uv"]])
    common = [rows(kv, F32), rows(rope, F32), rows(conv_ch, F32)]
    if absorbed:
        outs = [rows(n_heads * HEAD_PAD, BF16)] + common + [rows(n_heads * kv, BF16)]
    else:
        outs = [transposed(n_heads * HEAD_PAD)] + common + [rows(n_heads * HEAD_PAD, BF16),
                                                            transposed(w["w_uv"].shape[1])]
    return pl.pallas_call(
        functools.partial(_proj_body, absorbed=absorbed, n_heads=n_heads),
        out_shape=[o[0] for o in outs],
        grid=(n // tm,),
        in_specs=[pl.BlockSpec((tm, d), row), pl.BlockSpec((tm, LANES), row), pl.BlockSpec((tm, LANES), row)]
                 + [_resident(a.shape) for a in weights],
        out_specs=[o[1] for o in outs],
        compiler_params=_cparams("parallel"),
        name="proj_abs" if absorbed else "proj",
    )(x, t1, t2, *weights)


def _attn_body(qt_ref, k_ref, vt_ref, o_ref, *, tq, v_dim, unroll):
    qi = pl.program_id(2)
    heads = (0, 1)
    qts = [qt_ref[0, hh * HEAD_PAD:(hh + 1) * HEAD_PAD, :] for hh in heads]

    def chunk(j):
        return pl.ds(pl.multiple_of(j * tq, tq), tq)

    def scores(j):
        return tuple(_dot(k_ref[0, chunk(j), hh * HEAD_PAD:(hh + 1) * HEAD_PAD], qts[hh]) for hh in heads)

    def softmax(sts, ms, ls, masked):
        ps, ms_new, ls_new, alphas = [], [], [], []
        for hh in heads:
            st = sts[hh]
            if masked:
                key = lax.broadcasted_iota(jnp.int32, st.shape, 0)
                qry = lax.broadcasted_iota(jnp.int32, st.shape, 1)
                st = jnp.where(key <= qry, st, NEG_BIG)
            m_new = jnp.maximum(ms[hh], jnp.max(st, axis=0, keepdims=True))
            p = jnp.exp2(st - m_new)
            alpha = jnp.exp2(ms[hh] - m_new)
            ps.append(p.astype(BF16))
            ms_new.append(m_new)
            ls_new.append(alpha * ls[hh] + jnp.sum(p, axis=0, keepdims=True))
            alphas.append(alpha)
        return tuple(ps), tuple(ms_new), tuple(ls_new), tuple(alphas)

    def accumulate(j, ps, alphas, accs):
        return tuple(alphas[hh] * accs[hh] + _dot(vt_ref[0, hh * v_dim:(hh + 1) * v_dim, chunk(j)], ps[hh])
                     for hh in heads)

    def group(j, carry, n, last_masked):
        ms, ls, accs = carry
        sts = [scores(j + i) for i in range(n)]
        for i in range(n):
            ps, ms, ls, alphas = softmax(sts[i], ms, ls, last_masked and i == n - 1)
            accs = accumulate(j + i, ps, alphas, accs)
        return ms, ls, accs

    both = lambda x: (x, x)
    carry = (both(jnp.full((1, tq), NEG_BIG, F32)), both(jnp.zeros((1, tq), F32)), both(jnp.zeros((v_dim, tq), F32)))
    carry = lax.fori_loop(0, qi // unroll, lambda jj, c: group(unroll * jj, c, unroll, False), carry)
    rest = qi % unroll
    tails = [functools.partial(group, n=r + 1, last_masked=True) for r in range(unroll)]
    _, ls, accs = lax.switch(rest, [lambda c, t=t: t(qi - rest, c) for t in tails], carry)
    o_ref[0] = jnp.concatenate([accs[0] / ls[0], accs[1] / ls[1]], axis=0).T.astype(o_ref.dtype)


def _attn_prompt(qt, k, vt, *, n_heads, v_dim):
    b, t, _ = k.shape
    tq = min(512, t)
    return pl.pallas_call(
        functools.partial(_attn_body, tq=tq, v_dim=v_dim, unroll=4),
        out_shape=jax.ShapeDtypeStruct((b, t, n_heads * v_dim), BF16),
        grid=(b, n_heads // 2, t // tq),
        in_specs=[pl.BlockSpec((1, 2 * HEAD_PAD, tq), lambda bi, hp, qi: (bi, hp, qi)),
                  pl.BlockSpec((1, t, 2 * HEAD_PAD), lambda bi, hp, qi: (bi, 0, hp)),
                  pl.BlockSpec((1, 2 * v_dim, t), lambda bi, hp, qi: (bi, hp, 0))],
        out_specs=pl.BlockSpec((1, tq, 2 * v_dim), lambda bi, hp, qi: (bi, qi, hp)),
        compiler_params=_cparams("parallel", "parallel", "arbitrary"),
        name="attn_prompt",
    )(qt, k, vt)


def _paged_body(pt_ref, qa_ref, qr_ref, cn_ref, kn_ref, lat_hbm, krt_hbm, o_ref, lat_buf, krt_buf, sems,
                *, e, n_pages, n_heads):
    b = pl.program_id(0)
    page = krt_hbm.shape[3]

    def page_copies(seq, slot):
        copies = []
        for pi in range(n_pages):
            pid = pt_ref[seq * n_pages + pi]
            rows = pl.ds(pi * page, page)
            copies.append(pltpu.make_async_copy(lat_hbm.at[e, pid], lat_buf.at[slot, rows], sems.at[0, slot]))
            copies.append(pltpu.make_async_copy(krt_hbm.at[e, pid], krt_buf.at[slot, :, rows], sems.at[1, slot]))
        return copies

    @pl.when(b == 0)
    def _():
        for cp in page_copies(0, 0):
            cp.start()

    @pl.when(b + 1 < pl.num_programs(0))
    def _():
        for cp in page_copies(b + 1, (b + 1) % 2):
            cp.start()

    slot = b % 2
    for cp in page_copies(b, slot):
        cp.wait()

    qa = qa_ref[0]
    qr = qr_ref[0]
    lat = lat_buf[slot].astype(BF16)
    krt = krt_buf[slot].astype(BF16)
    cn = cn_ref[0].astype(BF16)
    kn = kn_ref[0].astype(BF16)
    s = _dot_nt(qa, lat) + _dot(qr, krt)
    sn = _dot_nt(qa, cn) + _dot_nt(qr, kn)
    r = lax.broadcasted_iota(jnp.int32, sn.shape, 0) // n_heads
    c = lax.broadcasted_iota(jnp.int32, sn.shape, 1)
    sn = jnp.where(c <= r, sn, NEG_BIG)
    m = jnp.maximum(jnp.max(s, axis=1, keepdims=True), jnp.max(sn, axis=1, keepdims=True))
    p = jnp.exp2(s - m)
    pn = jnp.exp2(sn - m)
    l = jnp.sum(p, axis=1, keepdims=True) + jnp.sum(pn, axis=1, keepdims=True)
    o = _dot(p.astype(BF16), lat) + _dot(pn.astype(BF16), cn)
    o_ref[0] = (o / l).astype(o_ref.dtype)


def _attn_paged(e, page_table, qa, qr, c_new, k_new, cache_latent, cache_krope_t, *, n_heads):
    db, rows, kv = qa.shape
    rope = qr.shape[2]
    n_pages = page_table.shape[1]
    page = cache_latent.shape[2]
    past = n_pages * page
    new_pad = c_new.shape[1]
    per_b = lambda b, pt: (b, 0, 0)
    grid_spec = pltpu.PrefetchScalarGridSpec(
        num_scalar_prefetch=1,
        grid=(db,),
        in_specs=[pl.BlockSpec((1, rows, kv), per_b), pl.BlockSpec((1, rows, rope), per_b),
                  pl.BlockSpec((1, new_pad, kv), per_b), pl.BlockSpec((1, new_pad, rope), per_b),
                  pl.BlockSpec(memory_space=pl.ANY), pl.BlockSpec(memory_space=pl.ANY)],
        out_specs=pl.BlockSpec((1, rows, kv), per_b),
        scratch_shapes=[pltpu.VMEM((2, past, kv), F32), pltpu.VMEM((2, rope, past), F32),
                        pltpu.SemaphoreType.DMA((2, 2))],
    )
    return pl.pallas_call(
        functools.partial(_paged_body, e=e, n_pages=n_pages, n_heads=n_heads),
        out_shape=jax.ShapeDtypeStruct((db, rows, kv), BF16),
        grid_spec=grid_spec,
        compiler_params=_cparams("arbitrary"),
        name="attn_paged",
    )(page_table.reshape(-1), qa, qr, c_new, k_new, cache_latent, cache_krope_t)


def _uv_body(o_ref, w_ref, a_ref, *, n_heads):
    a_ref[...] = jnp.concatenate([_dot(o_ref[i], w_ref[i]) for i in range(n_heads)], axis=1).astype(a_ref.dtype)


def _uv_proj(o_hm, w_uv_h):
    n_heads, n, kv = o_hm.shape
    v_dim = w_uv_h.shape[2]
    return pl.pallas_call(
        functools.partial(_uv_body, n_heads=n_heads),
        out_shape=jax.ShapeDtypeStruct((n, n_heads * v_dim), BF16),
        grid=(1,),
        in_specs=[_resident(o_hm.shape), _resident(w_uv_h.shape)],
        out_specs=pl.BlockSpec((n, n_heads * v_dim), lambda i: (0, 0)),
        compiler_params=_cparams("arbitrary"),
        name="uv_proj",
    )(o_hm, w_uv_h)


def _conv_body(x_ref, att_ref, u_ref, hist_ref, cw_ref, cb_ref, lg_ref, lb_ref, woa_ref, woc_ref,
               o_ref, st_ref, win_ref, *shift_refs, stride, width, sub):
    ti = pl.program_id(1)
    nt = pl.num_programs(1)
    tt = u_ref.shape[1]
    halo = hist_ref.shape[1]
    pad = halo // stride - (width - 1)

    @pl.when(ti == 0)
    def _():
        win_ref[0:halo, :] = hist_ref[0]

    win_ref[halo:halo + tt, :] = u_ref[0]

    offsets = [(kk + pad) * stride for kk in range(width)]
    shifts = _conv_shifts(offsets)
    for sh, ref in zip(shifts, shift_refs):
        ref[...] = win_ref[sh:sh + ref.shape[0], :]

    def tap(off, r0):
        sh = off % SUBLANES
        src = win_ref if sh == 0 else shift_refs[shifts.index(sh)]
        return src[r0 + off - sh:r0 + off - sh + sub, :]

    cb = cb_ref[...]
    lg = lg_ref[...]
    lb = lb_ref[...]
    cvs = []
    for r0 in range(0, tt, sub):
        y = jnp.zeros((sub, u_ref.shape[2]), F32)
        for kk in range(width):
            y = y + cw_ref[kk:kk + 1, :] * tap(offsets[kk], r0)
        y = y + cb
        yc = y - jnp.mean(y, axis=-1, keepdims=True)
        var = jnp.mean(yc * yc, axis=-1, keepdims=True)
        z = yc * lax.rsqrt(var + EPS) * lg + lb
        cvs.append((z * jax.nn.sigmoid(z)).astype(BF16))
    cv = jnp.concatenate(cvs, axis=0)
    o_ref[0] = x_ref[0] + _dot(att_ref[0], woa_ref[...]) + _dot(cv, woc_ref[...])

    @pl.when(ti == nt - 1)
    def _():
        st_ref[0] = win_ref[tt + pad * stride:tt + halo, :]

    if halo <= tt:
        @pl.when(ti < nt - 1)
        def _():
            win_ref[0:halo, :] = win_ref[tt:tt + halo, :]


def _conv_shifts(offsets):
    return sorted({off % SUBLANES for off in offsets} - {0})


def _conv_mix(x, att, u, hist, w, *, stride):
    bo, t, d = x.shape
    c = u.shape[2]
    a = att.shape[2]
    halo = hist.shape[1]
    width = w["conv_w"].shape[0]
    tt = min(512, t)
    assert t == tt or halo <= tt
    keep = (width - 1) * stride
    pad = halo // stride - (width - 1)
    n_shift = len(_conv_shifts([(kk + pad) * stride for kk in range(width)]))
    tile = lambda b, i: (b, i, 0)
    per_b = lambda b, i: (b, 0, 0)
    weights = [w["conv_w"], w["conv_b"], w["conv_ln_g"], w["conv_ln_b"], w["w_out_att"], w["w_out_conv"]]
    return pl.pallas_call(
        functools.partial(_conv_body, stride=stride, width=width, sub=min(64, tt)),
        out_shape=[jax.ShapeDtypeStruct((bo, t, d), F32), jax.ShapeDtypeStruct((bo, keep, c), F32)],
        grid=(bo, t // tt),
        in_specs=[pl.BlockSpec((1, tt, d), tile), pl.BlockSpec((1, tt, a), tile), pl.BlockSpec((1, tt, c), tile),
                  pl.BlockSpec((1, halo, c), per_b)] + [_resident(v.shape) for v in weights],
        out_specs=[pl.BlockSpec((1, tt, d), tile), pl.BlockSpec((1, keep, c), per_b)],
        scratch_shapes=[pltpu.VMEM((halo + tt, c), F32)] + [pltpu.VMEM((halo + tt - SUBLANES, c), F32)] * n_shift,
        compiler_params=_cparams("parallel", "arbitrary"),
        name="conv_mix",
    )(x, att, u, hist, *weights)


def _pool_body(x_ref, hist_ref, g_ref, pw_ref, ps_ref, o_ref, st_ref, win_ref, lvl_ref, *, stride, windows, pos0):
    ti = pl.program_id(1)
    nt = pl.num_programs(1)
    tt = x_ref.shape[1]
    halo = hist_ref.shape[1]
    d = x_ref.shape[2]
    grp = d // len(windows)

    @pl.when(ti == 0)
    def _():
        win_ref[0:halo, :] = hist_ref[0]

    x = x_ref[0]
    h = _rms(x, g_ref[...])
    win_ref[halo:halo + tt, :] = h

    pos = pos0 + ti * (tt // stride) + lax.broadcasted_iota(jnp.int32, (tt, 1), 0) // stride
    ys = []
    for gi, wn in enumerate(windows):
        cols = slice(gi * grp, (gi + 1) * grp)
        base = halo - (wn - 1) * stride
        n = tt + (wn - 2) * stride
        lvl_ref[0:n, cols] = win_ref[base + stride:base + stride + n, cols] + win_ref[base:base + n, cols]
        k = 2
        while k < wn:
            n -= k * stride
            lvl_ref[0:n, cols] = lvl_ref[k * stride:k * stride + n, cols] + lvl_ref[0:n, cols]
            k *= 2
        s = lvl_ref[0:tt, cols]
        cnt = jnp.minimum(pos + 1, wn).astype(F32)
        diff = (s / cnt - h[:, cols]).astype(BF16)
        ys.append(_dot(diff, pw_ref[gi]))
    o_ref[0] = x + jnp.concatenate(ys, axis=1) * ps_ref[...]

    keep = st_ref.shape[1]

    @pl.when(ti == nt - 1)
    def _():
        st_ref[0] = win_ref[tt + halo - keep:tt + halo, :]

    if halo <= tt:
        @pl.when(ti < nt - 1)
        def _():
            win_ref[0:halo, :] = win_ref[tt:tt + halo, :]


def _pool_mix(x, hist, g, pool_w, pool_scale, *, stride, windows, pos0):
    bo, t, d = x.shape
    halo = hist.shape[1]
    tt = min(512, t)
    assert t == tt or halo <= tt
    keep = (max(windows) - 1) * stride
    tile = lambda b, i: (b, i, 0)
    per_b = lambda b, i: (b, 0, 0)
    return pl.pallas_call(
        functools.partial(_pool_body, stride=stride, windows=windows, pos0=pos0),
        out_shape=[jax.ShapeDtypeStruct((bo, t, d), F32), jax.ShapeDtypeStruct((bo, keep, d), F32)],
        grid=(bo, t // tt),
        in_specs=[pl.BlockSpec((1, tt, d), tile), pl.BlockSpec((1, halo, d), per_b),
                  _resident(g.shape), _resident(pool_w.shape), _resident(pool_scale.shape)],
        out_specs=[pl.BlockSpec((1, tt, d), tile), pl.BlockSpec((1, keep, d), per_b)],
        scratch_shapes=[pltpu.VMEM((halo + tt, d), F32), pltpu.VMEM((halo + tt, d), F32)],
        compiler_params=_cparams("parallel", "arbitrary"),
        name="pool_mix",
    )(x, hist, g, pool_w, pool_scale)


def _rope_tables(pos, nope, rope):
    inv = ROPE_BASE ** (-jnp.arange(0, rope, 2, dtype=F32) / rope)
    ang = pos[:, None] * inv[None, :]
    cos, sin = jnp.cos(ang), jnp.sin(ang)
    n = pos.shape[0]
    zero_hi = jnp.zeros((n, HEAD_PAD - nope - rope), F32)
    t1 = jnp.concatenate([jnp.ones((n, nope), F32), cos, cos, zero_hi], axis=1)
    t2 = jnp.concatenate([jnp.zeros((n, nope), F32), -sin, sin, zero_hi], axis=1)
    return t1, t2


def _time_major(a):
    b, s, c = a.shape
    return a.transpose(1, 0, 2).reshape(1, s * b, c)


def _batch_major(a, b):
    c = a.shape[2]
    return a.reshape(-1, b, c).transpose(1, 0, 2)


def kernel(x_prompt, x_sample, cache_latent, cache_krope, state_conv, state_pool, page_table, ffn1_norm, ffn1_w_gate, ffn1_w_up, ffn1_w_down, mix_norm, w_in, q_norm, w_uq, kv_norm, w_uk, w_uv, conv_w, conv_b, conv_ln_g, conv_ln_b, w_out, pool_w, pool_scale, ffn2_norm, ffn2_w_gate, ffn2_w_up, ffn2_w_down, final_norm):
    B, T, D = x_prompt.shape
    DB, TS, _ = x_sample.shape
    depth = ffn1_norm.shape[0]
    q_lora, n_heads, qk_dim = w_uq.shape[1:]
    kv_lora, _, nope = w_uk.shape[1:]
    v_dim = w_uv.shape[3]
    rope = qk_dim - nope
    conv_ch = conv_w.shape[2]
    conv_width = conv_w.shape[1]
    n_groups = pool_w.shape[1]
    grp = D // n_groups
    windows = tuple(2 ** (i + 1) for i in range(n_groups))
    pool_state = state_pool.shape[2]
    n_pages = page_table.shape[1]
    page = cache_latent.shape[2]
    past_len = n_pages * page
    scale = float(qk_dim) ** -0.5 * LOG2_E
    assert nope == 64 and rope == 32 and v_dim == 64 and max(windows) - 1 == pool_state and grp * n_groups == D

    t1_p, t2_p = _rope_tables(jnp.arange(T, dtype=F32), nope, rope)
    t1_p, t2_p = jnp.tile(t1_p, (B, 1)), jnp.tile(t2_p, (B, 1))
    t1_s, t2_s = _rope_tables(past_len + jnp.arange(TS, dtype=F32), nope, rope)
    t1_s, t2_s = jnp.tile(t1_s, (DB, 1)), jnp.tile(t2_s, (DB, 1))

    def pad_rope_cols(wr, lead):
        z0 = jnp.zeros(wr.shape[:-1] + (lead,), wr.dtype)
        z1 = jnp.zeros(wr.shape[:-1] + (HEAD_PAD - lead - rope,), wr.dtype)
        return jnp.concatenate([z0, wr, z1], axis=-1)

    def even_weights(l):
        e = l // 2
        wi = w_in[e]
        o = 0
        w_q = wi[:, o:o + q_lora]; o += q_lora
        w_kv = wi[:, o:o + kv_lora]; o += kv_lora
        w_kr = wi[:, o:o + rope]; o += rope
        w_a = wi[:, o:o + conv_ch]; o += conv_ch
        w_g = wi[:, o:o + conv_ch]
        uq = w_uq[e] * scale
        uq = jnp.concatenate([uq, jnp.zeros((q_lora, n_heads, HEAD_PAD - qk_dim), F32)], axis=-1)
        uk = w_uk[e]
        uk_pad = jnp.concatenate([uk, jnp.zeros((kv_lora, n_heads, HEAD_PAD - nope), F32)], axis=-1)
        uk_abs = jnp.concatenate([uk.transpose(1, 2, 0),
                                  jnp.zeros((n_heads, HEAD_PAD - nope, kv_lora), F32)], axis=1)
        return dict(
            n_heads=n_heads, rope=rope,
            mix_norm=mix_norm[l][None], w_q=w_q.astype(BF16), w_kv=w_kv.astype(BF16),
            w_kr=pad_rope_cols(w_kr, nope).astype(BF16), w_a=w_a.astype(BF16), w_g=w_g.astype(BF16),
            q_norm=q_norm[e][None], kv_norm=kv_norm[e][None],
            w_uq=uq.reshape(q_lora, n_heads * HEAD_PAD).astype(BF16),
            w_uk_pad=uk_pad.reshape(kv_lora, n_heads * HEAD_PAD).astype(BF16),
            w_uk_abs=uk_abs.astype(BF16),
            w_uv=w_uv[e].reshape(kv_lora, n_heads * v_dim).astype(BF16),
            w_uv_h=w_uv[e].transpose(1, 0, 2).astype(BF16),
            conv_w=conv_w[e], conv_b=conv_b[e][None], conv_ln_g=conv_ln_g[e][None], conv_ln_b=conv_ln_b[e][None],
            w_out_att=w_out[e][:n_heads * v_dim].astype(BF16), w_out_conv=w_out[e][n_heads * v_dim:].astype(BF16),
        )

    cache_krope_t = jnp.swapaxes(cache_krope, 2, 3)
    xp = x_prompt.reshape(B * T, D)
    xs = x_sample.reshape(DB * TS, D)
    lat_p, kr_p, lat_s, kr_s, conv_p, conv_s, pool_p, pool_s = ([] for _ in range(8))
    conv_halo = 32 if conv_width - 1 <= 32 else conv_width - 1
    pool_halo = pool_state + 1

    for l in range(depth):
        fg = final_norm[None]
        w1 = (ffn1_norm, ffn1_w_gate, ffn1_w_up, ffn1_w_down)
        xp, xs = _ffn(xp, xs, *w1, fg, l)
        if l % 2 == 0:
            e = l // 2
            w = even_weights(l)
            qt_p, c_p, k_rope_p, u_p, k_p, vt_p = _proj(xp, t1_p, t2_p, w, absorbed=False, seqs=B)
            q_s, c_s, k_rope_s, u_s, qa_s = _proj(xs, t1_s, t2_s, w, absorbed=True)

            att_p = _attn_prompt(qt_p, k_p.reshape(B, T, -1), vt_p, n_heads=n_heads, v_dim=v_dim)

            qa = qa_s.reshape(DB, TS * n_heads, kv_lora)
            qr = q_s.reshape(DB, TS * n_heads, HEAD_PAD)[:, :, nope:nope + rope]
            new_pad = 16
            c_new = jnp.pad(c_s.reshape(DB, TS, kv_lora), ((0, 0), (0, new_pad - TS), (0, 0)))
            k_new = jnp.pad(k_rope_s.reshape(DB, TS, rope), ((0, 0), (0, new_pad - TS), (0, 0)))
            o_s = _attn_paged(e, page_table, qa, qr, c_new, k_new, cache_latent, cache_krope_t, n_heads=n_heads)
            o_hm = o_s.reshape(DB * TS, n_heads, kv_lora).transpose(1, 0, 2)
            att_s = _uv_proj(o_hm, w["w_uv_h"])

            xp3, st_p = _conv_mix(xp.reshape(B, T, D), att_p, u_p.reshape(B, T, conv_ch),
                                  jnp.zeros((B, conv_halo, conv_ch), F32), w, stride=1)
            xp = xp3.reshape(B * T, D)
            hist_s = jnp.pad(state_conv[e], ((0, 0), (conv_halo - (conv_width - 1), 0), (0, 0)))
            xs3, st_s = _conv_mix(_time_major(xs.reshape(DB, TS, D)), _time_major(att_s.reshape(DB, TS, -1)),
                                  _time_major(u_s.reshape(DB, TS, conv_ch)), _time_major(hist_s), w, stride=DB)
            xs = _batch_major(xs3, DB).reshape(DB * TS, D)

            lat_p.append(c_p.reshape(B, T, kv_lora)); kr_p.append(k_rope_p.reshape(B, T, rope))
            lat_s.append(c_s.reshape(DB, TS, kv_lora)); kr_s.append(k_rope_s.reshape(DB, TS, rope))
            conv_p.append(st_p); conv_s.append(_batch_major(st_s, DB))
        else:
            o = l // 2
            g = mix_norm[l][None]
            pw = pool_w[o].astype(BF16)
            ps = pool_scale[o][None]
            xp3, st_p = _pool_mix(xp.reshape(B, T, D), jnp.zeros((B, pool_halo, D), F32), g, pw, ps,
                                  stride=1, windows=windows, pos0=0)
            xp = xp3.reshape(B * T, D)
            hist_s = jnp.pad(state_pool[o], ((0, 0), (pool_halo - pool_state, 0), (0, 0)))
            xs3, st_s = _pool_mix(_time_major(xs.reshape(DB, TS, D)), _time_major(hist_s), g, pw, ps,
                                  stride=DB, windows=windows, pos0=past_len)
            xs = _batch_major(xs3, DB).reshape(DB * TS, D)
            pool_p.append(st_p); pool_s.append(_batch_major(st_s, DB))
        w2 = (ffn2_norm, ffn2_w_gate, ffn2_w_up, ffn2_w_down)
        last = l == depth - 1
        xp, xs = _ffn(xp, xs, *w2, fg, l, final=last)

    return (xp.reshape(B, T, D), xs.reshape(DB, TS, D),
            jnp.stack(lat_p), jnp.stack(kr_p), jnp.stack(lat_s), jnp.stack(kr_s),
            jnp.stack(conv_p), jnp.stack(conv_s), jnp.stack(pool_p), jnp.stack(pool_s))
```
